```python
import jax, jax.numpy as jnp
from jax import lax
import numpy as np

D_MODEL = 1024
BATCH = 8
SEQ = 2048
DEPTH = 4

CTX_LEN = 256
GRID_W = 64
HEAD_DIM = 64
A_HEADS = 4
A_KV_HEADS = 2
A_WINDOW = 128
A_BLOCK = 128
B_HEADS = 4
B_WIN_H = 8
B_WIN_W = 16
C_HEADS = 4
C_Q_RANK = 256
C_KV_RANK = 128
C_NOPE = 64
C_ROPE = 32
C_V = 64
C_BLOCK = 128
D_GROUPS = 4
D_CHUNK = 128
A_W = A_HEADS * HEAD_DIM
A_KV_W = A_KV_HEADS * HEAD_DIM
B_W = B_HEADS * HEAD_DIM
C_W = C_HEADS * C_V
D_WIDTH = D_GROUPS * HEAD_DIM
MIX_W = A_W + B_W + C_W + D_WIDTH
IN_SIZES = (A_KV_W, A_KV_W, B_W, B_W, C_KV_RANK, C_ROPE, A_W, B_W, C_Q_RANK, D_WIDTH, D_WIDTH)
KV_COLS = 2 * A_KV_W + 2 * B_W + C_KV_RANK + C_ROPE
IN_W = KV_COLS + A_W + B_W + C_Q_RANK + 2 * D_WIDTH
FF_HIDDEN = -(-8 * D_MODEL // (3 * 256)) * 256
ROPE_BASE = 10000.0
LN_EPS = 1e-6
NEG_INF = -1e30
DN_ALPHA = (2 * DEPTH) ** 0.25
DN_BETA = (8 * DEPTH) ** -0.25

kernel_name = "hybrid_parallel_group_diffusion_trunk"


def _layer_norm(x, g, b):
    xf = x.astype(jnp.float32)
    xc = xf - jnp.mean(xf, -1, keepdims=True)
    var = jnp.mean(xc * xc, -1, keepdims=True)
    return (xc * lax.rsqrt(var + LN_EPS) * g.astype(jnp.float32) + b.astype(jnp.float32)).astype(x.dtype)


def _rms_norm(x, g):
    xf = x.astype(jnp.float32)
    return (xf * lax.rsqrt(jnp.mean(xf * xf, -1, keepdims=True) + LN_EPS) * g.astype(jnp.float32)).astype(x.dtype)


def _rope_1d(x, pos):
    d = x.shape[-1]
    inv = ROPE_BASE ** (-jnp.arange(0, d, 2, dtype=jnp.float32) / d)
    ang = pos.astype(jnp.float32)[:, None] * inv[None, :]
    cos = jnp.cos(ang)[None, :, None, :]
    sin = jnp.sin(ang)[None, :, None, :]
    x1, x2 = jnp.split(x.astype(jnp.float32), 2, axis=-1)
    return jnp.concatenate([x1 * cos - x2 * sin, x1 * sin + x2 * cos], -1).astype(x.dtype)


def _axial_rope(x, rows, cols):
    half = x.shape[-1] // 2
    return jnp.concatenate([_rope_1d(x[..., :half], rows), _rope_1d(x[..., half:], cols)], -1)


def _heads(t, n, d=HEAD_DIM):
    return t.reshape(t.shape[:-1] + (n, d))


def _dense_attn(q, k, v, scale):
    s = jnp.einsum('bqhd,bkhd->bhqk', q, k).astype(jnp.float32) * scale
    p = jax.nn.softmax(s, axis=-1).astype(v.dtype)
    return jnp.einsum('bhqk,bkhd->bqhd', p, v)


def _mixer_a(q, k, v, kc, vc, qc, sink, rows, cols):
    bsz, seq = q.shape[:2]
    nctx = kc.shape[1]
    grp = A_HEADS // A_KV_HEADS
    nb = seq // A_BLOCK
    nl = 3 * A_BLOCK
    scale = HEAD_DIM ** -0.5
    q = _axial_rope(q, rows, cols)
    k = _axial_rope(k, rows, cols)
    qb = q.reshape(bsz, nb, A_BLOCK, A_KV_HEADS, grp, HEAD_DIM)

    def band(t):
        tp = jnp.pad(t, ((0, 0), (A_BLOCK, A_BLOCK), (0, 0), (0, 0))).reshape(bsz, nb + 2, A_BLOCK, A_KV_HEADS, HEAD_DIM)
        return jnp.concatenate([tp[:, :-2], tp[:, 1:-1], tp[:, 2:]], axis=2)

    kb, vb = band(k), band(v)
    qpos = jnp.arange(seq).reshape(nb, A_BLOCK)
    kpos = (jnp.arange(nb)[:, None] - 1) * A_BLOCK + jnp.arange(nl)[None, :]
    valid = ((jnp.abs(qpos[:, :, None] - kpos[:, None, :]) <= A_WINDOW)
             & (kpos[:, None, :] >= 0) & (kpos[:, None, :] < seq))
    s_loc = jnp.einsum('bnqhgd,bnkhd->bnhgqk', qb, kb).astype(jnp.float32) * scale
    s_loc = jnp.where(valid[None, :, None, None], s_loc, NEG_INF)
    s_ctx = jnp.einsum('bnqhgd,bkhd->bnhgqk', qb, kc).astype(jnp.float32) * scale
    sink_f = sink.astype(jnp.float32).reshape(A_KV_HEADS, grp, 1, 1)
    s_sink = jnp.broadcast_to(sink_f, s_ctx.shape[:-1] + (1,))
    p = jax.nn.softmax(jnp.concatenate([s_loc, s_ctx, s_sink], -1), axis=-1).astype(v.dtype)
    o = (jnp.einsum('bnhgqk,bnkhd->bnqhgd', p[..., :nl], vb)
         + jnp.einsum('bnhgqk,bkhd->bnqhgd', p[..., nl:nl + nctx], vc))
    y = o.reshape(bsz, seq, A_W)
    if qc is None:
        return y, None
    qcg = qc.reshape(bsz, nctx, A_KV_HEADS, grp, HEAD_DIM)
    sc = jnp.einsum('bqhgd,bkhd->bhgqk', qcg, kc).astype(jnp.float32) * scale
    pc = jax.nn.softmax(jnp.concatenate([sc, jnp.broadcast_to(sink_f, sc.shape[:-1] + (1,))], -1), axis=-1)
    yc = jnp.einsum('bhgqk,bkhd->bqhgd', pc[..., :nctx].astype(v.dtype), vc).reshape(bsz, nctx, A_W)
    return y, yc


def _mixer_b(q, k, v, kc, vc, qc, rpb):
    bsz, seq = q.shape[:2]
    n_rows = seq // GRID_W
    wh = min(B_WIN_H, n_rows)
    scale = HEAD_DIM ** -0.5
    qg = q.reshape(bsz, n_rows, GRID_W, B_HEADS, HEAD_DIM)
    kg = k.reshape(bsz, n_rows, GRID_W, B_HEADS, HEAD_DIM)
    vg = v.reshape(bsz, n_rows, GRID_W, B_HEADS, HEAD_DIM)
    r = jnp.arange(n_rows)
    r_start = jnp.clip(r - wh // 2, 0, n_rows - wh)
    row_idx = r_start[:, None] + jnp.arange(wh)[None, :]
    kn = kg[:, row_idx]
    vn = vg[:, row_idx]
    cidx = jnp.arange(GRID_W)
    c_start = jnp.clip(cidx - B_WIN_W // 2, 0, GRID_W - B_WIN_W)
    col_ok = (cidx[None, :] >= c_start[:, None]) & (cidx[None, :] < c_start[:, None] + B_WIN_W)
    dr_i = row_idx - r[:, None] + (B_WIN_H - 1)
    dc_i = jnp.clip(cidx[None, :] - cidx[:, None] + (B_WIN_W - 1), 0, 2 * B_WIN_W - 2)
    bias = rpb.astype(jnp.float32)[:, dr_i[:, None, :, None], dc_i[None, :, None, :]]
    bias = jnp.transpose(bias, (1, 0, 2, 3, 4))
    s = jnp.einsum('brqhd,brjkhd->brhqjk', qg, kn).astype(jnp.float32) * scale + bias[None]
    s = jnp.where(col_ok[:, None, :], s, NEG_INF).reshape(bsz, n_rows, B_HEADS, GRID_W, wh * GRID_W)
    s_ctx = jnp.einsum('brqhd,bkhd->brhqk', qg, kc).astype(jnp.float32) * scale
    p = jax.nn.softmax(jnp.concatenate([s, s_ctx], -1), axis=-1).astype(v.dtype)
    p_loc = p[..., :wh * GRID_W].reshape(bsz, n_rows, B_HEADS, GRID_W, wh, GRID_W)
    o = (jnp.einsum('brhqjk,brjkhd->brqhd', p_loc, vn)
         + jnp.einsum('brhqk,bkhd->brqhd', p[..., wh * GRID_W:], vc))
    y = o.reshape(bsz, seq, B_W)
    if qc is None:
        return y, None
    yc = _dense_attn(qc, kc, vc, scale).reshape(bsz, kc.shape[1], B_W)
    return y, yc


def _mixer_c(cq, ckv, kr, ckv_c, kr_c, cq_c, q_norm, kv_norm, w_uq, w_ukv, rows, cols):
    bsz, seq = cq.shape[:2]
    nctx = ckv_c.shape[1]
    scale = (C_NOPE + C_ROPE) ** -0.5

    def queries(t):
        qh = (_rms_norm(t, q_norm) @ w_uq).reshape(t.shape[:2] + (C_HEADS, C_NOPE + C_ROPE))
        return qh[..., :C_NOPE], qh[..., C_NOPE:]

    def keys_values(t, rk):
        kv = (_rms_norm(t, kv_norm) @ w_ukv).reshape(t.shape[:2] + (C_HEADS, C_NOPE + C_V))
        rk_h = jnp.broadcast_to(rk[:, :, None, :], rk.shape[:2] + (C_HEADS, C_ROPE))
        return jnp.concatenate([kv[..., :C_NOPE], rk_h], -1), kv[..., C_NOPE:]

    k_lat, v_lat = keys_values(ckv, _axial_rope(kr[:, :, None, :], rows, cols)[:, :, 0])
    k_ctx, v_ctx = keys_values(ckv_c, kr_c)
    k_all = jnp.concatenate([k_ctx, k_lat], axis=1)
    v_all = jnp.concatenate([v_ctx, v_lat], axis=1)
    qn, qr = queries(cq)
    q = jnp.concatenate([qn, _axial_rope(qr, rows, cols)], -1)
    nb = seq // C_BLOCK
    qb = jnp.moveaxis(q.reshape(bsz, nb, C_BLOCK, C_HEADS, C_NOPE + C_ROPE), 1, 0)
    o = lax.map(lambda qblk: _dense_attn(qblk, k_all, v_all, scale), qb)
    y = jnp.moveaxis(o, 0, 1).reshape(bsz, seq, C_W)
    if cq_c is None:
        return y, None
    qnc, qrc = queries(cq_c)
    yc = _dense_attn(jnp.concatenate([qnc, qrc], -1), k_ctx, v_ctx, scale).reshape(bsz, nctx, C_W)
    return y, yc


def _mixer_d(du, dv, ln_g, ln_b, ws, bs):
    bsz, length = du.shape[:2]
    nc = length // D_CHUNK
    u = jax.nn.gelu(du)
    v = _layer_norm(jax.nn.gelu(dv), ln_g, ln_b)
    vc = v.reshape(bsz, nc, D_CHUNK, D_GROUPS, HEAD_DIM)
    mixed = jnp.einsum('gts,bcsgd->bctgd', ws, vc) + bs.T[:, :, None]
    return u * mixed.reshape(bsz, length, D_WIDTH)


def _swiglu(h, w_gu, w_down):
    g, u = jnp.split(h @ w_gu, 2, axis=-1)
    return (jax.nn.silu(g) * u) @ w_down


def setup_inputs(seed: int = 0) -> dict:
    key = jax.random.key(seed)
    ks = jax.random.split(key, 24)

    def nrm(k, shape, s):
        return jax.random.normal(k, shape, jnp.float32) * s

    L = DEPTH
    return {
        "x": nrm(ks[0], (BATCH, SEQ, D_MODEL), 1.0),
        "c": nrm(ks[1], (BATCH, D_MODEL), 1.0),
        "ctx": nrm(ks[2], (BATCH, CTX_LEN, D_MODEL), 1.0),
        "c_ctx": nrm(ks[3], (D_MODEL,), 1.0),
        "w_mod": nrm(ks[4], (L, D_MODEL, 6 * D_MODEL), 0.5 * D_MODEL ** -0.5),
        "b_mod": nrm(ks[5], (L, 6 * D_MODEL), 0.02),
        "w_in": nrm(ks[6], (L, D_MODEL, IN_W), D_MODEL ** -0.5),
        "a_sink": nrm(ks[7], (L, A_HEADS), 0.5),
        "b_rpb": nrm(ks[8], (L, B_HEADS, 2 * B_WIN_H - 1, 2 * B_WIN_W - 1), 0.2),
        "c_q_norm": 1.0 + nrm(ks[9], (L, C_Q_RANK), 0.02),
        "c_kv_norm": 1.0 + nrm(ks[10], (L, C_KV_RANK), 0.02),
        "c_w_uq": nrm(ks[11], (L, C_Q_RANK, C_HEADS * (C_NOPE + C_ROPE)), C_Q_RANK ** -0.5),
        "c_w_ukv": nrm(ks[12], (L, C_KV_RANK, C_HEADS * (C_NOPE + C_V)), C_KV_RANK ** -0.5),
        "d_ln_g": 1.0 + nrm(ks[13], (L, D_WIDTH), 0.02),
        "d_ln_b": nrm(ks[14], (L, D_WIDTH), 0.02),
        "d_ws": nrm(ks[15], (L, D_GROUPS, D_CHUNK, D_CHUNK), D_CHUNK ** -0.5),
        "d_bs": 1.0 + nrm(ks[16], (L, D_GROUPS, D_CHUNK), 0.02),
        "w_out": nrm(ks[17], (L, MIX_W, D_MODEL), DN_BETA * MIX_W ** -0.5),
        "ln1_g": 1.0 + nrm(ks[18], (L, D_MODEL), 0.02),
        "ln1_b": nrm(ks[19], (L, D_MODEL), 0.02),
        "w_gu": nrm(ks[20], (L, D_MODEL, 2 * FF_HIDDEN), D_MODEL ** -0.5),
        "w_down": nrm(ks[21], (L, FF_HIDDEN, D_MODEL), DN_BETA * FF_HIDDEN ** -0.5),
        "ln2_g": 1.0 + nrm(ks[22], (L, D_MODEL), 0.02),
        "ln2_b": nrm(ks[23], (L, D_MODEL), 0.02),
    }


def reference(x, c, ctx, c_ctx, w_mod, b_mod, w_in, a_sink, b_rpb, c_q_norm, c_kv_norm, c_w_uq, c_w_ukv,
              d_ln_g, d_ln_b, d_ws, d_bs, w_out, ln1_g, ln1_b, w_gu, w_down, ln2_g, ln2_b):
    seq = x.shape[1]
    t = jnp.arange(seq)
    rows, cols = t // GRID_W, t % GRID_W
    split_at = np.cumsum(IN_SIZES)[:-1].tolist()
    s_c = jax.nn.silu(c)
    s_cc = jax.nn.silu(c_ctx)
    for l in range(DEPTH):
        last = l == DEPTH - 1
        m = jnp.split((s_c @ w_mod[l] + b_mod[l])[:, None, :], 6, axis=-1)
        mc = jnp.split(s_cc @ w_mod[l] + b_mod[l], 6, axis=-1)
        h = x * (1 + m[1]) + m[0]
        hc = ctx * (1 + mc[1]) + mc[0]
        ak, av, bk, bv, cckv, ckr, aq, bq, ccq, du, dv = jnp.split(h @ w_in[l], split_at, axis=-1)
        if last:
            akc, avc, bkc, bvc, cckvc, ckrc = jnp.split(hc @ w_in[l][:, :KV_COLS], split_at[:5], axis=-1)
            aqc = bqc = ccqc = None
        else:
            akc, avc, bkc, bvc, cckvc, ckrc, aqc, bqc, ccqc, duc, dvc = jnp.split(hc @ w_in[l], split_at, axis=-1)
            aqc, bqc = _heads(aqc, A_HEADS), _heads(bqc, B_HEADS)
        ya, yac = _mixer_a(_heads(aq, A_HEADS), _heads(ak, A_KV_HEADS), _heads(av, A_KV_HEADS),
                           _heads(akc, A_KV_HEADS), _heads(avc, A_KV_HEADS), aqc, a_sink[l], rows, cols)
        yb, ybc = _mixer_b(_heads(bq, B_HEADS), _heads(bk, B_HEADS), _heads(bv, B_HEADS),
                           _heads(bkc, B_HEADS), _heads(bvc, B_HEADS), bqc, b_rpb[l])
        yc, ycc = _mixer_c(ccq, cckv, ckr, cckvc, ckrc, ccqc, c_q_norm[l], c_kv_norm[l],
                           c_w_uq[l], c_w_ukv[l], rows, cols)
        yd = _mixer_d(du, dv, d_ln_g[l], d_ln_b[l], d_ws[l], d_bs[l])
        y = jnp.concatenate([ya, yb, yc, yd], axis=-1) @ w_out[l]
        x = _layer_norm(DN_ALPHA * x + m[2] * y, ln1_g[l], ln1_b[l])
        x = _layer_norm(DN_ALPHA * x + m[5] * _swiglu(x * (1 + m[4]) + m[3], w_gu[l], w_down[l]),
                        ln2_g[l], ln2_b[l])
        if not last:
            ydc = _mixer_d(duc, dvc, d_ln_g[l], d_ln_b[l], d_ws[l], d_bs[l])
            y_ctx = jnp.concatenate([yac, ybc, ycc, ydc], axis=-1) @ w_out[l]
            ctx = _layer_norm(DN_ALPHA * ctx + mc[2] * y_ctx, ln1_g[l], ln1_b[l])
            ctx = _layer_norm(DN_ALPHA * ctx + mc[5] * _swiglu(ctx * (1 + mc[4]) + mc[3], w_gu[l], w_down[l]),
                              ln2_g[l], ln2_b[l])
    return x
```

```python
import functools

import numpy as np
import jax
import jax.numpy as jnp
from jax import lax
from jax.experimental import pallas as pl
from jax.experimental.pallas import tpu as pltpu

D_MODEL = 1024
BATCH = 8
SEQ = 2048
DEPTH = 4
CTX_LEN = 256
GRID_W = 64
HEAD_DIM = 64
A_WINDOW = 128
B_WIN_H = 8
B_WIN_W = 16
C_Q_RANK = 256
C_KV_RANK = 128
C_NOPE = 64
C_ROPE = 32
FF_HIDDEN = 2816
ROPE_BASE = 10000.0
LN_EPS = 1e-6
NEG_INF = -1e30
DN_ALPHA = (2 * DEPTH) ** 0.25

N_ROWS = SEQ // GRID_W
NX = BATCH * SEQ
NC = BATCH * CTX_LEN
T = NX + NC

LANES = 128
VMEM_LIMIT = 56 * 1024 * 1024

TM = 256
X_TILES = NX // TM
C_TILES = NC // TM
TILES_PER_SEQ = SEQ // TM
A_BLK = 128
A_BAND = 3 * A_BLK
B_QROWS = 2
B_QBLK = B_QROWS * GRID_W
B_KROWS = B_WIN_H + B_QROWS
B_KBLK = B_KROWS * GRID_W
B_STEPS = N_ROWS // B_QROWS
B_CLASSES = 5
C_QBLK = 256
FF_CHUNK = 256

P_AQ, P_BK, P_BV, P_BQ, P_CKX0, P_CKX1, P_CV, P_CQN, P_YD = (256 * i for i in range(9))
P_AK, P_AV, P_CQR = 2304, 2432, 2560
P_W = 2688
W1_AQ, W1_BK, W1_BV, W1_BQ, W1_CCQ, W1_DU, W1_DV, W1_AKV, W1_CKV = (256 * i for i in range(9))
W1_W = 2304


def _bf(x):
    return x.astype(jnp.bfloat16)


def _dot(a, b):
    return jnp.dot(a, b, preferred_element_type=jnp.float32)


def _dot_nt(a, b):
    return lax.dot_general(a, b, (((1,), (1,)), ((), ())), preferred_element_type=jnp.float32)


def _lane_lt(n, limit):
    return lax.broadcasted_iota(jnp.int32, (1, n), 1) < limit


def _half_mask(c):
    lane = lax.broadcasted_iota(jnp.int32, (1, LANES), 1)
    return (lane >= c * HEAD_DIM) & (lane < (c + 1) * HEAD_DIM)


def _gelu(x):
    return x * (0.5 * (1.0 + jnp.tanh(np.sqrt(2.0 / np.pi).astype(np.float32) * (x + 0.044715 * (x * x * x)))))


def _layer_norm(x, g, b):
    xc = x - jnp.mean(x, axis=-1, keepdims=True)
    var = jnp.mean(xc * xc, axis=-1, keepdims=True)
    return xc * lax.rsqrt(var + LN_EPS) * g + b


def _rms_norm(x, g):
    return x * lax.rsqrt(jnp.mean(x * x, axis=-1, keepdims=True) + LN_EPS) * g


def _rope(x, cos, sin_signed, half):
    lane = lax.broadcasted_iota(jnp.int32, (1, LANES), 1)
    first = (lane % (2 * half)) < half
    partner = jnp.where(first, pltpu.roll(x, LANES - half, 1), pltpu.roll(x, half, 1))
    return x * cos + partner * sin_signed


def _softmax_parts(scores, extra=None):
    m = functools.reduce(jnp.maximum, [jnp.max(s, axis=-1, keepdims=True) for s in scores])
    if extra is not None:
        m = jnp.maximum(m, extra)
    ps = [jnp.exp(s - m) for s in scores]
    l = functools.reduce(lambda a, b: a + b, [jnp.sum(p, axis=-1, keepdims=True) for p in ps])
    if extra is not None:
        l = l + jnp.exp(extra - m)
    return ps, l


MOD_TN = 1536


def _mod_kernel(c_ref, w_ref, b_ref, o_ref):
    c = c_ref[...]
    s = c * (1.0 / (1.0 + jnp.exp(-c)))
    o_ref[...] = _dot(_bf(s), _bf(w_ref[...])) + b_ref[...]


def _modulation(c16, w_mod, b_mod):
    n = 6 * D_MODEL
    return pl.pallas_call(
        _mod_kernel,
        grid=(DEPTH, n // MOD_TN),
        in_specs=[
            pl.BlockSpec((16, D_MODEL), lambda l, j: (0, 0)),
            pl.BlockSpec((None, D_MODEL, MOD_TN), lambda l, j: (l, 0, j)),
            pl.BlockSpec((None, 1, MOD_TN), lambda l, j: (l, 0, j)),
        ],
        out_specs=pl.BlockSpec((None, 16, MOD_TN), lambda l, j: (l, 0, j)),
        out_shape=jax.ShapeDtypeStruct((DEPTH, 16, n), jnp.float32),
        compiler_params=pltpu.CompilerParams(vmem_limit_bytes=VMEM_LIMIT),
        name="modulation",
    )(c16, w_mod, b_mod.reshape(DEPTH, 1, n))


def _b_step_geometry(step):
    r0 = step * B_QROWS
    w0 = min(max(r0 - B_WIN_H // 2, 0), N_ROWS - B_KROWS)
    return r0, w0


_B_CLASS_STEPS = (0, 1, 2, B_STEPS - 2, B_STEPS - 1)


def _bias_kernel(rpb_ref, o_ref):
    n_dr, n_dc = 2 * B_WIN_H - 1, 2 * B_WIN_W - 1
    cq = lax.broadcasted_iota(jnp.int32, (GRID_W, LANES), 0)
    ck = lax.broadcasted_iota(jnp.int32, (GRID_W, LANES), 1) % GRID_W
    c_start = jnp.clip(cq - B_WIN_W // 2, 0, GRID_W - B_WIN_W)
    col_ok = (ck >= c_start) & (ck < c_start + B_WIN_W)
    dc = jnp.clip(ck - cq + (B_WIN_W - 1), 0, n_dc - 1)
    neg = jnp.full((GRID_W, LANES), NEG_INF, jnp.float32)
    lane_lo = lax.broadcasted_iota(jnp.int32, (GRID_W, LANES), 1) < GRID_W
    for h in range(4):
        tz = []
        for dr in range(n_dr):
            t = jnp.zeros((GRID_W, LANES), jnp.float32)
            for d in range(n_dc):
                t = jnp.where(dc == d, rpb_ref[(h * n_dr + dr) * n_dc + d], t)
            tz.append(jnp.where(col_ok, t, neg))
        for ci, step in enumerate(_B_CLASS_STEPS):
            r0, w0 = _b_step_geometry(step)
            for qr in range(B_QROWS):
                r = r0 + qr
                r_start = min(max(r - B_WIN_H // 2, 0), N_ROWS - B_WIN_H)
                for kp in range(B_KROWS // 2):
                    halves = []
                    for kr in (2 * kp, 2 * kp + 1):
                        rk = w0 + kr
                        ok = r_start <= rk < r_start + B_WIN_H
                        halves.append(tz[rk - r + (B_WIN_H - 1)] if ok else neg)
                    o_ref[ci, h, qr * GRID_W:(qr + 1) * GRID_W, kp * LANES:(kp + 1) * LANES] = (
                        jnp.where(lane_lo, halves[0], halves[1]))


def _bias_tables(rpb_l):
    return pl.pallas_call(
        _bias_kernel,
        in_specs=[pl.BlockSpec(memory_space=pltpu.SMEM)],
        out_specs=pl.BlockSpec(memory_space=pltpu.VMEM),
        out_shape=jax.ShapeDtypeStruct((B_CLASSES, 4, B_QBLK, B_KBLK), jnp.float32),
        compiler_params=pltpu.CompilerParams(vmem_limit_bytes=VMEM_LIMIT),
        name="b_bias_tables",
    )(rpb_l.reshape(-1))


def _in_kernel(x_ref, mod_ref, w1_ref, cosa_ref, sina_ref, cosc_ref, sinc_ref, qn_ref, kvn_ref,
               wuq_ref, wukv_ref, dg_ref, db_ref, ws_ref, bs_ref, p_ref):
    shift = mod_ref[:, 0:D_MODEL]
    scale = mod_ref[:, D_MODEL:2 * D_MODEL]
    h = _bf(x_ref[...] * (1.0 + scale) + shift)

    def proj(off):
        return _dot(h, w1_ref[:, off:off + 256])

    cosa, sina = cosa_ref[...], sina_ref[...]
    cosc, sinc = cosc_ref[...], sinc_ref[...]
    qscale = HEAD_DIM ** -0.5

    aq = proj(W1_AQ)
    for t in range(2):
        sl = slice(t * LANES, (t + 1) * LANES)
        p_ref[:, P_AQ + t * LANES:P_AQ + (t + 1) * LANES] = _bf(_rope(aq[:, sl], cosa, sina, 16) * qscale)
    p_ref[:, P_BK:P_BK + 256] = _bf(proj(W1_BK))
    p_ref[:, P_BV:P_BV + 256] = _bf(proj(W1_BV))
    p_ref[:, P_BQ:P_BQ + 256] = _bf(proj(W1_BQ) * qscale)
    akv = proj(W1_AKV)
    p_ref[:, P_AK:P_AK + LANES] = _bf(_rope(akv[:, :LANES], cosa, sina, 16))
    p_ref[:, P_AV:P_AV + LANES] = _bf(akv[:, LANES:])

    ckv = proj(W1_CKV)
    kr4 = _bf(_rope(ckv[:, LANES:], cosc, sinc, 8))
    kv = _dot(_bf(_rms_norm(ckv[:, :LANES], kvn_ref[...])), wukv_ref[...])
    p_ref[:, P_CKX0:P_CKX0 + LANES] = _bf(kv[:, 0:LANES])
    p_ref[:, P_CKX0 + LANES:P_CKX0 + 256] = kr4
    p_ref[:, P_CKX1:P_CKX1 + LANES] = _bf(kv[:, LANES:2 * LANES])
    p_ref[:, P_CKX1 + LANES:P_CKX1 + 256] = kr4
    p_ref[:, P_CV:P_CV + 256] = _bf(kv[:, 256:512])
    qq = _dot(_bf(_rms_norm(proj(W1_CCQ), qn_ref[...])), wuq_ref[...])
    p_ref[:, P_CQN:P_CQN + 256] = _bf(qq[:, 0:256])
    p_ref[:, P_CQR:P_CQR + LANES] = _bf(_rope(qq[:, 256:384], cosc, sinc, 8))

    u = _gelu(proj(W1_DU))
    v = _bf(_layer_norm(_gelu(proj(W1_DV)), dg_ref[...], db_ref[...]))
    lo = _lane_lt(LANES, HEAD_DIM)
    for ch in range(TM // 128):
        rows = slice(ch * 128, (ch + 1) * 128)
        for pr in range(2):
            cols = slice(pr * LANES, (pr + 1) * LANES)
            vt = v[rows, cols]
            mixed = jnp.where(lo, _dot(ws_ref[2 * pr], vt), _dot(ws_ref[2 * pr + 1], vt))
            p_ref[rows, P_YD + pr * LANES:P_YD + (pr + 1) * LANES] = _bf(
                u[rows, cols] * (mixed + bs_ref[:, cols]))


def _mod_row(i):
    return jnp.minimum(i // TILES_PER_SEQ, BATCH)


def _pos_block(i):
    return jnp.where(i < X_TILES, i % TILES_PER_SEQ, TILES_PER_SEQ + (i - X_TILES) % (CTX_LEN // TM))


def _stage_in(l, xc, mods, w1, tabs, qn, kvn, wuq, wukv, dg, db, ws, bs_exp):
    full = lambda shape: pl.BlockSpec((None,) + shape, lambda i: (l,) + (0,) * len(shape))
    tab = pl.BlockSpec((TM, LANES), lambda i: (_pos_block(i), 0))
    return pl.pallas_call(
        _in_kernel,
        grid=(T // TM,),
        in_specs=[
            pl.BlockSpec((TM, D_MODEL), lambda i: (i, 0)),
            pl.BlockSpec((None, None, 1, 6 * D_MODEL), lambda i: (l, _mod_row(i), 0, 0)),
            full((D_MODEL, W1_W)),
            tab, tab, tab, tab,
            full((1, C_Q_RANK)), full((1, C_KV_RANK)),
            full((C_Q_RANK, 384)), full((C_KV_RANK, 512)),
            full((1, 256)), full((1, 256)),
            full((4, 128, 128)), full((128, 256)),
        ],
        out_specs=pl.BlockSpec((TM, P_W), lambda i: (i, 0)),
        out_shape=jax.ShapeDtypeStruct((T, P_W), jnp.bfloat16),
        compiler_params=pltpu.CompilerParams(vmem_limit_bytes=VMEM_LIMIT),
        name="stage_in",
    )(xc, mods, w1, *tabs, qn, kvn, wuq, wukv, dg, db, ws, bs_exp)


def _attend(q_tile, c, score_blocks, value_blocks, bias=None, valid=None, extra=None, scale=None):
    scores = []
    for i, kt in enumerate(score_blocks):
        s = _dot_nt(q_tile, kt)
        if scale is not None:
            s = s * scale
        if i == 0 and bias is not None:
            s = s + bias
        if i == 0 and valid is not None:
            s = jnp.where(valid, s, NEG_INF)
        scores.append(s)
    ps, l = _softmax_parts(scores, extra)
    o = functools.reduce(lambda a, b: a + b, [_dot(_bf(p), vt) for p, vt in zip(ps, value_blocks)])
    return o / l


def _mask_half(q_tile, c):
    return jnp.where(_half_mask(c), q_tile, jnp.zeros_like(q_tile))


def _a_kernel(sink_ref, q_ref, k_ref, v_ref, kc_ref, vc_ref, o_ref):
    n = pl.program_id(1)
    start = pl.multiple_of(jnp.clip((n - 1) * A_BLK, 0, SEQ - A_BAND), A_BLK)
    kw = k_ref[pl.ds(start, A_BAND), :]
    vw = v_ref[pl.ds(start, A_BAND), :]
    kc, vc = kc_ref[...], vc_ref[...]
    qpos = n * A_BLK + lax.broadcasted_iota(jnp.int32, (A_BLK, A_BAND), 0)
    kpos = start + lax.broadcasted_iota(jnp.int32, (A_BLK, A_BAND), 1)
    valid = jnp.abs(qpos - kpos) <= A_WINDOW
    lo = _lane_lt(LANES, HEAD_DIM)
    for t in range(2):
        q_tile = q_ref[:, t * LANES:(t + 1) * LANES]
        outs = []
        for c in range(2):
            sink = sink_ref[t + 2 * c]
            outs.append(_attend(_mask_half(q_tile, c), c, (kw, kc), (vw, vc), valid=valid,
                                extra=jnp.full((A_BLK, 1), sink, jnp.float32)))
        o_ref[:, t * LANES:(t + 1) * LANES] = _bf(jnp.where(lo, outs[0], outs[1]))


def _mixer_a(p, sink_l):
    nb = SEQ // A_BLK
    cb = NX // CTX_LEN
    return pl.pallas_call(
        _a_kernel,
        grid=(BATCH, nb),
        in_specs=[
            pl.BlockSpec(memory_space=pltpu.SMEM),
            pl.BlockSpec((A_BLK, 256), lambda b, n: (b * nb + n, P_AQ // 256)),
            pl.BlockSpec((SEQ, LANES), lambda b, n: (b, P_AK // LANES)),
            pl.BlockSpec((SEQ, LANES), lambda b, n: (b, P_AV // LANES)),
            pl.BlockSpec((CTX_LEN, LANES), lambda b, n: (cb + b, P_AK // LANES)),
            pl.BlockSpec((CTX_LEN, LANES), lambda b, n: (cb + b, P_AV // LANES)),
        ],
        out_specs=pl.BlockSpec((A_BLK, 256), lambda b, n: (b * nb + n, 0)),
        out_shape=jax.ShapeDtypeStruct((T, 256), jnp.bfloat16),
        compiler_params=pltpu.CompilerParams(vmem_limit_bytes=VMEM_LIMIT),
        name="mixer_a",
    )(sink_l, p, p, p, p, p)


def _b_kernel(q_ref, k_ref, v_ref, kc_ref, vc_ref, bias_ref, o_ref):
    step = pl.program_id(1)
    w0 = jnp.clip(step * B_QROWS - B_WIN_H // 2, 0, N_ROWS - B_KROWS)
    start = pl.multiple_of(w0 * GRID_W, GRID_W)
    lo = _lane_lt(LANES, HEAD_DIM)
    for t in range(2):
        cols = slice(t * LANES, (t + 1) * LANES)
        q_tile = q_ref[:, cols]
        kw = k_ref[pl.ds(start, B_KBLK), cols]
        vw = v_ref[pl.ds(start, B_KBLK), cols]
        kc, vc = kc_ref[:, cols], vc_ref[:, cols]
        outs = [_attend(_mask_half(q_tile, c), c, (kw, kc), (vw, vc), bias=bias_ref[2 * t + c])
                for c in range(2)]
        o_ref[:, cols] = _bf(jnp.where(lo, outs[0], outs[1]))


def _b_class(step):
    return jnp.where(step < 2, step, jnp.where(step >= B_STEPS - 2, step - (B_STEPS - B_CLASSES), 2))


def _mixer_b(p, bias_tab):
    cb = NX // CTX_LEN
    return pl.pallas_call(
        _b_kernel,
        grid=(BATCH, B_STEPS),
        in_specs=[
            pl.BlockSpec((B_QBLK, 256), lambda b, s: (b * B_STEPS + s, P_BQ // 256)),
            pl.BlockSpec((SEQ, 256), lambda b, s: (b, P_BK // 256)),
            pl.BlockSpec((SEQ, 256), lambda b, s: (b, P_BV // 256)),
            pl.BlockSpec((CTX_LEN, 256), lambda b, s: (cb + b, P_BK // 256)),
            pl.BlockSpec((CTX_LEN, 256), lambda b, s: (cb + b, P_BV // 256)),
            pl.BlockSpec((None, 4, B_QBLK, B_KBLK), lambda b, s: (_b_class(s), 0, 0, 0)),
        ],
        out_specs=pl.BlockSpec((B_QBLK, 256), lambda b, s: (b * B_STEPS + s, 0)),
        out_shape=jax.ShapeDtypeStruct((T, 256), jnp.bfloat16),
        compiler_params=pltpu.CompilerParams(vmem_limit_bytes=VMEM_LIMIT),
        name="mixer_b",
    )(p, p, p, p, p, bias_tab)


C_SCALE = (C_NOPE + C_ROPE) ** -0.5


def _c_query(qn_tile, qr_tile, h):
    lane = lax.broadcasted_iota(jnp.int32, (1, LANES), 1)
    quarter = (lane >= h * C_ROPE) & (lane < (h + 1) * C_ROPE)
    return jnp.concatenate([_mask_half(qn_tile, h % 2),
                            jnp.where(quarter, qr_tile, jnp.zeros_like(qr_tile))], axis=1)


def _c_kernel(qn_ref, qr_ref, kx0_ref, kx1_ref, v_ref, kx0c_ref, kx1c_ref, vc_ref, o_ref):
    lo = _lane_lt(LANES, HEAD_DIM)
    qr = qr_ref[...]
    for t, (kx_ref, kxc_ref) in enumerate(((kx0_ref, kx0c_ref), (kx1_ref, kx1c_ref))):
        cols = slice(t * LANES, (t + 1) * LANES)
        qn_tile = qn_ref[:, cols]
        outs = [_attend(_c_query(qn_tile, qr, 2 * t + c), c, (kx_ref[...], kxc_ref[...]),
                        (v_ref[:, cols], vc_ref[:, cols]), scale=C_SCALE) for c in range(2)]
        o_ref[:, cols] = _bf(jnp.where(lo, outs[0], outs[1]))


def _mixer_c(p):
    nq = SEQ // C_QBLK
    cb = NX // CTX_LEN
    return pl.pallas_call(
        _c_kernel,
        grid=(BATCH, nq),
        in_specs=[
            pl.BlockSpec((C_QBLK, 256), lambda b, n: (b * nq + n, P_CQN // 256)),
            pl.BlockSpec((C_QBLK, LANES), lambda b, n: (b * nq + n, P_CQR // LANES)),
            pl.BlockSpec((SEQ, 256), lambda b, n: (b, P_CKX0 // 256)),
            pl.BlockSpec((SEQ, 256), lambda b, n: (b, P_CKX1 // 256)),
            pl.BlockSpec((SEQ, 256), lambda b, n: (b, P_CV // 256)),
            pl.BlockSpec((CTX_LEN, 256), lambda b, n: (cb + b, P_CKX0 // 256)),
            pl.BlockSpec((CTX_LEN, 256), lambda b, n: (cb + b, P_CKX1 // 256)),
            pl.BlockSpec((CTX_LEN, 256), lambda b, n: (cb + b, P_CV // 256)),
        ],
        out_specs=pl.BlockSpec((C_QBLK, 256), lambda b, n: (b * nq + n, 0)),
        out_shape=jax.ShapeDtypeStruct((T, 256), jnp.bfloat16),
        compiler_params=pltpu.CompilerParams(vmem_limit_bytes=VMEM_LIMIT),
        name="mixer_c",
    )(p, p, p, p, p, p, p, p)


def _ctx_kernel(sink_ref, aq_ref, ak_ref, av_ref, bq_ref, bk_ref, bv_ref, cqn_ref, cqr_ref,
                kx0_ref, kx1_ref, cv_ref, ya_in, yb_in, yc_in, ya_ref, yb_ref, yc_ref):
    del ya_in, yb_in, yc_in
    lo = _lane_lt(LANES, HEAD_DIM)
    ak, av = ak_ref[...], av_ref[...]
    cqr = cqr_ref[...]
    for t in range(2):
        cols = slice(t * LANES, (t + 1) * LANES)
        aq_tile = aq_ref[:, cols]
        outs = [_attend(_mask_half(aq_tile, c), c, (ak,), (av,),
                        extra=jnp.full((CTX_LEN, 1), sink_ref[t + 2 * c], jnp.float32)) for c in range(2)]
        ya_ref[:, cols] = _bf(jnp.where(lo, outs[0], outs[1]))
        bq_tile = bq_ref[:, cols]
        outs = [_attend(_mask_half(bq_tile, c), c, (bk_ref[:, cols],), (bv_ref[:, cols],)) for c in range(2)]
        yb_ref[:, cols] = _bf(jnp.where(lo, outs[0], outs[1]))
        kx = (kx0_ref, kx1_ref)[t][...]
        cqn_tile = cqn_ref[:, cols]
        outs = [_attend(_c_query(cqn_tile, cqr, 2 * t + c), c, (kx,), (cv_ref[:, cols],), scale=C_SCALE)
                for c in range(2)]
        yc_ref[:, cols] = _bf(jnp.where(lo, outs[0], outs[1]))


def _mixers_ctx(p, sink_l, ya, yb, yc):
    cb = NX // CTX_LEN
    blk = lambda w, off: pl.BlockSpec((CTX_LEN, w), lambda b: (cb + b, off // w))
    anyspec = pl.BlockSpec(memory_space=pl.ANY)
    yspec = pl.BlockSpec((CTX_LEN, 256), lambda b: (cb + b, 0))
    yshape = jax.ShapeDtypeStruct((T, 256), jnp.bfloat16)
    return pl.pallas_call(
        _ctx_kernel,
        grid=(BATCH,),
        in_specs=[
            pl.BlockSpec(memory_space=pltpu.SMEM),
            blk(256, P_AQ), blk(LANES, P_AK), blk(LANES, P_AV),
            blk(256, P_BQ), blk(256, P_BK), blk(256, P_BV),
            blk(256, P_CQN), blk(LANES, P_CQR), blk(256, P_CKX0), blk(256, P_CKX1), blk(256, P_CV),
            anyspec, anyspec, anyspec,
        ],
        out_specs=[yspec, yspec, yspec],
        out_shape=[yshape, yshape, yshape],
        input_output_aliases={12: 0, 13: 1, 14: 2},
        compiler_params=pltpu.CompilerParams(vmem_limit_bytes=VMEM_LIMIT),
        name="mixers_ctx",
    )(sink_l, *([p] * 11), ya, yb, yc)


def _post_kernel(x_ref, mod_ref, ya_ref, yb_ref, yc_ref, yd_ref, wo_ref, g1_ref, b1_ref,
                 wgu_ref, wd_ref, g2_ref, b2_ref, o_ref):
    m = lambda j: mod_ref[:, j * D_MODEL:(j + 1) * D_MODEL]
    ycat = jnp.concatenate([ya_ref[...], yb_ref[...], yc_ref[...], yd_ref[...]], axis=1)
    y = _dot(ycat, wo_ref[...])
    x1 = _layer_norm(DN_ALPHA * x_ref[...] + m(2) * y, g1_ref[...], b1_ref[...])
    h = _bf(x1 * (1.0 + m(4)) + m(3))
    acc = jnp.zeros((TM, D_MODEL), jnp.float32)
    for j in range(FF_HIDDEN // FF_CHUNK):
        g = _dot(h, wgu_ref[:, j * FF_CHUNK:(j + 1) * FF_CHUNK])
        u = _dot(h, wgu_ref[:, FF_HIDDEN + j * FF_CHUNK:FF_HIDDEN + (j + 1) * FF_CHUNK])
        a = _bf(g * (1.0 / (1.0 + jnp.exp(-g))) * u)
        acc = acc + _dot(a, wd_ref[j * FF_CHUNK:(j + 1) * FF_CHUNK, :])
    o_ref[...] = _layer_norm(DN_ALPHA * x1 + m(5) * acc, g2_ref[...], b2_ref[...])


def _stage_post(l, n_tiles, xc, mods, ya, yb, yc, p, wo, g1, b1, wgu, wd, g2, b2):
    full = lambda shape: pl.BlockSpec((None,) + shape, lambda i: (l,) + (0,) * len(shape))
    yspec = pl.BlockSpec((TM, 256), lambda i: (i, 0))
    return pl.pallas_call(
        _post_kernel,
        grid=(n_tiles,),
        in_specs=[
            pl.BlockSpec((TM, D_MODEL), lambda i: (i, 0)),
            pl.BlockSpec((None, None, 1, 6 * D_MODEL), lambda i: (l, _mod_row(i), 0, 0)),
            yspec, yspec, yspec,
            pl.BlockSpec((TM, 256), lambda i: (i, P_YD // 256)),
            full((D_MODEL, D_MODEL)), full((1, D_MODEL)), full((1, D_MODEL)),
            full((D_MODEL, 2 * FF_HIDDEN)), full((FF_HIDDEN, D_MODEL)),
            full((1, D_MODEL)), full((1, D_MODEL)),
        ],
        out_specs=pl.BlockSpec((TM, D_MODEL), lambda i: (i, 0)),
        out_shape=jax.ShapeDtypeStruct((n_tiles * TM, D_MODEL), jnp.float32),
        compiler_params=pltpu.CompilerParams(vmem_limit_bytes=VMEM_LIMIT),
        name="stage_post",
    )(xc, mods, ya, yb, yc, p, wo, g1, b1, wgu, wd, g2, b2)


def _rope_tables():
    t = jnp.arange(SEQ)
    pos = jnp.stack([t // GRID_W, t % GRID_W], 0).astype(jnp.float32)

    def table(width):
        half, quarter = width // 2, width // 4
        inv = ROPE_BASE ** (-jnp.arange(0, half, 2, dtype=jnp.float32) / half)
        lane = np.arange(LANES) % width
        axis = lane // half
        k = (lane % half) % quarter
        first = (lane % half) < quarter
        ang = pos[axis, :].T * inv[k][None, :]
        cos = jnp.cos(ang)
        sin = jnp.where(first[None, :], -jnp.sin(ang), jnp.sin(ang))
        ident = (jnp.ones((CTX_LEN, LANES), jnp.float32), jnp.zeros((CTX_LEN, LANES), jnp.float32))
        return jnp.concatenate([cos, ident[0]], 0), jnp.concatenate([sin, ident[1]], 0)

    cosa, sina = table(HEAD_DIM)
    cosc, sinc = table(C_ROPE)
    return cosa, sina, cosc, sinc


def _w1_columns():
    ak, av, bk, bv, cckv, ckr, aq, bq, ccq, du, dv = np.cumsum((0, 128, 128, 256, 256, 128, 32, 256, 256, 256, 256))[:11]
    r = lambda off, n: np.arange(off, off + n)
    aq_cols = np.concatenate([r(aq + 64 * h, 64) for h in (0, 2, 1, 3)])
    return np.concatenate([aq_cols, r(bk, 256), r(bv, 256), r(bq, 256), r(ccq, 256), r(du, 256), r(dv, 256),
                           r(ak, 128), r(av, 128), r(cckv, 128), np.tile(r(ckr, 32), 4)])


def kernel(x, c, ctx, c_ctx, w_mod, b_mod, w_in, a_sink, b_rpb, c_q_norm, c_kv_norm, c_w_uq, c_w_ukv,
           d_ln_g, d_ln_b, d_ws, d_bs, w_out, ln1_g, ln1_b, w_gu, w_down, ln2_g, ln2_b):
    f32 = jnp.float32
    xc = jnp.concatenate([x.reshape(NX, D_MODEL), ctx.reshape(NC, D_MODEL)], axis=0)
    c16 = jnp.concatenate([c, c_ctx[None, :], jnp.zeros((16 - BATCH - 1, D_MODEL), f32)], axis=0)
    mods = _modulation(c16, w_mod, b_mod).reshape(DEPTH, 16, 1, 6 * D_MODEL)

    w1 = _bf(w_in[:, :, _w1_columns()])
    uq_cols = np.concatenate([np.arange(96 * h, 96 * h + 64) for h in range(4)]
                             + [np.arange(96 * h + 64, 96 * h + 96) for h in range(4)])
    ukv_cols = np.concatenate([np.arange(128 * h, 128 * h + 64) for h in range(4)]
                              + [np.arange(128 * h + 64, 128 * h + 128) for h in range(4)])
    wuq = _bf(c_w_uq[:, :, uq_cols])
    wukv = _bf(c_w_ukv[:, :, ukv_cols])
    wo_rows = np.concatenate([np.arange(64 * h, 64 * h + 64) for h in (0, 2, 1, 3)] + [np.arange(256, 1024)])
    wo = _bf(w_out[:, wo_rows, :])
    wgu = _bf(w_gu)
    wd = _bf(w_down)
    ws = _bf(d_ws)
    bs_exp = jnp.repeat(jnp.swapaxes(d_bs, 1, 2), HEAD_DIM, axis=2)
    row = lambda a: a[:, None, :]
    tabs = _rope_tables()

    for l in range(DEPTH):
        last = l == DEPTH - 1
        p = _stage_in(l, xc, mods, w1, tabs, row(c_q_norm), row(c_kv_norm), wuq, wukv,
                      row(d_ln_g), row(d_ln_b), ws, bs_exp)
        ya = _mixer_a(p, a_sink[l])
        yb = _mixer_b(p, _bias_tables(b_rpb[l]))
        yc = _mixer_c(p)
        if not last:
            ya, yb, yc = _mixers_ctx(p, a_sink[l], ya, yb, yc)
        xc = _stage_post(l, X_TILES if last else T // TM, xc, mods, ya, yb, yc, p, wo,
                         row(ln1_g), row(ln1_b), wgu, wd, row(ln2_g), row(ln2_b))
    return xc.reshape(BATCH, SEQ, D_MODEL)
```

```python
import functools

import numpy as np
import jax
import jax.numpy as jnp
from jax import lax
from jax.experimental import pallas as pl
from jax.experimental.pallas import tpu as pltpu

D_MODEL = 1024
BATCH = 8
SEQ = 2048
DEPTH = 4
CTX_LEN = 256
GRID_W = 64
HEAD_DIM = 64
A_WINDOW = 128
B_WIN_H = 8
B_WIN_W = 16
C_Q_RANK = 256
C_KV_RANK = 128
C_NOPE = 64
C_ROPE = 32
FF_HIDDEN = 2816
ROPE_BASE = 10000.0
LN_EPS = 1e-6
NEG_INF = -1e30
DN_ALPHA = (2 * DEPTH) ** 0.25

N_ROWS = SEQ // GRID_W
NX = BATCH * SEQ
NC = BATCH * CTX_LEN
T = NX + NC

LANES = 128
VMEM_LIMIT = 56 * 1024 * 1024

TM = 256
X_TILES = NX // TM
C_TILES = NC // TM
TILES_PER_SEQ = SEQ // TM
A_BLK = 128
A_BAND = 3 * A_BLK
B_QROWS = 2
B_QBLK = B_QROWS * GRID_W
B_KROWS = B_WIN_H + B_QROWS
B_KBLK = B_KROWS * GRID_W
B_STEPS = N_ROWS // B_QROWS
B_CLASSES = 5
A_SUB = 8
B_SUB = 8
C_QBLK = 512
C_SUBBLK = 128
FF_CHUNK = 256

P_AQ, P_BK, P_BV, P_BQ, P_CKX0, P_CKX1, P_CV, P_CQN, P_YD = (256 * i for i in range(9))
P_AK, P_AV, P_CQR = 2304, 2432, 2560
P_W = 2688
W1_AQ, W1_BK, W1_BV, W1_BQ, W1_CCQ, W1_DU, W1_DV, W1_AKV, W1_CKV = (256 * i for i in range(9))
W1_W = 2304


def _bf(x):
    return x.astype(jnp.bfloat16)


def _dot(a, b):
    return jnp.dot(a, b, preferred_element_type=jnp.float32)


def _dot_nt(a, b):
    return lax.dot_general(a, b, (((1,), (1,)), ((), ())), preferred_element_type=jnp.float32)


def _lane_lt(n, limit):
    return lax.broadcasted_iota(jnp.int32, (1, n), 1) < limit


def _half_mask(c):
    lane = lax.broadcasted_iota(jnp.int32, (1, LANES), 1)
    return (lane >= c * HEAD_DIM) & (lane < (c + 1) * HEAD_DIM)


def _gelu(x):
    return x * (0.5 * (1.0 + jnp.tanh(np.sqrt(2.0 / np.pi).astype(np.float32) * (x + 0.044715 * (x * x * x)))))


def _layer_norm(x, g, b):
    xc = x - jnp.mean(x, axis=-1, keepdims=True)
    var = jnp.mean(xc * xc, axis=-1, keepdims=True)
    return xc * lax.rsqrt(var + LN_EPS) * g + b


def _rms_norm(x, g):
    return x * lax.rsqrt(jnp.mean(x * x, axis=-1, keepdims=True) + LN_EPS) * g


def _rope(x, cos, sin_signed, half):
    lane = lax.broadcasted_iota(jnp.int32, (1, LANES), 1)
    first = (lane % (2 * half)) < half
    partner = jnp.where(first, pltpu.roll(x, LANES - half, 1), pltpu.roll(x, half, 1))
    return x * cos + partner * sin_signed


def _softmax_num(scores, extra=None, base2=False):
    ex = jnp.exp2 if base2 else jnp.exp
    s = scores[0] if len(scores) == 1 else jnp.concatenate(scores, axis=1)
    m = jnp.max(s, axis=-1, keepdims=True)
    if extra is None:
        return _bf(ex(s - m)), None
    m = jnp.maximum(m, extra)
    return _bf(ex(s - m)), ex(extra - m)


def _pv_normalised(p, extra_num, values):
    v = values[0] if len(values) == 1 else jnp.concatenate(values, axis=0)
    oe = _dot(p, jnp.concatenate([v, jnp.ones_like(v)], axis=1))
    o, l = oe[:, :LANES], oe[:, LANES:]
    return o / (l if extra_num is None else l + extra_num)


def _softmax_pv(scores, values, extra=None, base2=False):
    return _pv_normalised(*_softmax_num(scores, extra, base2), values)


def _pipelined(n, stages):
    stages = [s if callable(s) else (lambda i, c, fs=s: functools.reduce(lambda acc, f: f(i, acc), fs, c))
              for s in stages]
    depth = len(stages)
    carry = [None] * n
    for tick in range(n + depth - 1):
        for k in range(depth):
            i = tick - k
            if 0 <= i < n:
                carry[i] = stages[k](i, carry[i])


MOD_TN = 1536


def _mod_kernel(c_ref, w_ref, b_ref, o_ref):
    c = c_ref[...]
    s = c * (1.0 / (1.0 + jnp.exp(-c)))
    o_ref[...] = _dot(_bf(s), _bf(w_ref[...])) + b_ref[...]


def _modulation(c16, w_mod, b_mod):
    n = 6 * D_MODEL
    return pl.pallas_call(
        _mod_kernel,
        grid=(DEPTH, n // MOD_TN),
        in_specs=[
            pl.BlockSpec((16, D_MODEL), lambda l, j: (0, 0)),
            pl.BlockSpec((None, D_MODEL, MOD_TN), lambda l, j: (l, 0, j)),
            pl.BlockSpec((None, 1, MOD_TN), lambda l, j: (l, 0, j)),
        ],
        out_specs=pl.BlockSpec((None, 16, MOD_TN), lambda l, j: (l, 0, j)),
        out_shape=jax.ShapeDtypeStruct((DEPTH, 16, n), jnp.float32),
        compiler_params=pltpu.CompilerParams(vmem_limit_bytes=VMEM_LIMIT),
        name="modulation",
    )(c16, w_mod, b_mod.reshape(DEPTH, 1, n))


def _b_step_geometry(step):
    r0 = step * B_QROWS
    w0 = min(max(r0 - B_WIN_H // 2, 0), N_ROWS - B_KROWS)
    return r0, w0


_B_CLASS_STEPS = (0, 1, 2, B_STEPS - 2, B_STEPS - 1)


def _bias_kernel(rpb_ref, o_ref):
    n_dr, n_dc = 2 * B_WIN_H - 1, 2 * B_WIN_W - 1
    cq = lax.broadcasted_iota(jnp.int32, (GRID_W, LANES), 0)
    ck = lax.broadcasted_iota(jnp.int32, (GRID_W, LANES), 1) % GRID_W
    c_start = jnp.clip(cq - B_WIN_W // 2, 0, GRID_W - B_WIN_W)
    col_ok = (ck >= c_start) & (ck < c_start + B_WIN_W)
    dc = jnp.clip(ck - cq + (B_WIN_W - 1), 0, n_dc - 1)
    neg = jnp.full((GRID_W, LANES), NEG_INF, jnp.float32)
    lane_lo = lax.broadcasted_iota(jnp.int32, (GRID_W, LANES), 1) < GRID_W
    for h in range(4):
        tz = []
        for dr in range(n_dr):
            t = jnp.zeros((GRID_W, LANES), jnp.float32)
            for d in range(n_dc):
                t = jnp.where(dc == d, rpb_ref[(h * n_dr + dr) * n_dc + d], t)
            tz.append(jnp.where(col_ok, t, neg))
        for ci, step in enumerate(_B_CLASS_STEPS):
            r0, w0 = _b_step_geometry(step)
            for qr in range(B_QROWS):
                r = r0 + qr
                r_start = min(max(r - B_WIN_H // 2, 0), N_ROWS - B_WIN_H)
                for kp in range(B_KROWS // 2):
                    halves = []
                    for kr in (2 * kp, 2 * kp + 1):
                        rk = w0 + kr
                        ok = r_start <= rk < r_start + B_WIN_H
                        halves.append(tz[rk - r + (B_WIN_H - 1)] if ok else neg)
                    o_ref[ci, h, qr * GRID_W:(qr + 1) * GRID_W, kp * LANES:(kp + 1) * LANES] = (
                        jnp.where(lane_lo, halves[0], halves[1]))


def _bias_tables(rpb_l):
    return pl.pallas_call(
        _bias_kernel,
        in_specs=[pl.BlockSpec(memory_space=pltpu.SMEM)],
        out_specs=pl.BlockSpec(memory_space=pltpu.VMEM),
        out_shape=jax.ShapeDtypeStruct((B_CLASSES, 4, B_QBLK, B_KBLK), jnp.float32),
        compiler_params=pltpu.CompilerParams(vmem_limit_bytes=VMEM_LIMIT),
        name="b_bias_tables",
    )(rpb_l.reshape(-1))


def _in_kernel(x_ref, mod_ref, w1_ref, cosa_ref, sina_ref, cosc_ref, sinc_ref, qn_ref, kvn_ref,
               wuq_ref, wukv_ref, dg_ref, db_ref, ws_ref, bs_ref, p_ref):
    shift = mod_ref[:, 0:D_MODEL]
    scale = mod_ref[:, D_MODEL:2 * D_MODEL]
    h = _bf(x_ref[...] * (1.0 + scale) + shift)

    def proj(off):
        return _dot(h, w1_ref[:, off:off + 256])

    cosa, sina = cosa_ref[...], sina_ref[...]
    cosc, sinc = cosc_ref[...], sinc_ref[...]
    qscale = HEAD_DIM ** -0.5

    aq = proj(W1_AQ)
    for t in range(2):
        sl = slice(t * LANES, (t + 1) * LANES)
        p_ref[:, P_AQ + t * LANES:P_AQ + (t + 1) * LANES] = _bf(_rope(aq[:, sl], cosa, sina, 16) * qscale)
    p_ref[:, P_BK:P_BK + 256] = _bf(proj(W1_BK))
    p_ref[:, P_BV:P_BV + 256] = _bf(proj(W1_BV))
    p_ref[:, P_BQ:P_BQ + 256] = _bf(proj(W1_BQ) * qscale)
    akv = proj(W1_AKV)
    p_ref[:, P_AK:P_AK + LANES] = _bf(_rope(akv[:, :LANES], cosa, sina, 16))
    p_ref[:, P_AV:P_AV + LANES] = _bf(akv[:, LANES:])

    ckv = proj(W1_CKV)
    kr4 = _bf(_rope(ckv[:, LANES:], cosc, sinc, 8))
    kv = _dot(_bf(_rms_norm(ckv[:, :LANES], kvn_ref[...])), wukv_ref[...])
    p_ref[:, P_CKX0:P_CKX0 + LANES] = _bf(kv[:, 0:LANES])
    p_ref[:, P_CKX0 + LANES:P_CKX0 + 256] = kr4
    p_ref[:, P_CKX1:P_CKX1 + LANES] = _bf(kv[:, LANES:2 * LANES])
    p_ref[:, P_CKX1 + LANES:P_CKX1 + 256] = kr4
    p_ref[:, P_CV:P_CV + 256] = _bf(kv[:, 256:512])
    qq = _dot(_bf(_rms_norm(proj(W1_CCQ), qn_ref[...])), wuq_ref[...])
    p_ref[:, P_CQN:P_CQN + 256] = _bf(qq[:, 0:256])
    p_ref[:, P_CQR:P_CQR + LANES] = _bf(_rope(qq[:, 256:384], cosc, sinc, 8))

    u = _gelu(proj(W1_DU))
    v = _bf(_layer_norm(_gelu(proj(W1_DV)), dg_ref[...], db_ref[...]))
    lo = _lane_lt(LANES, HEAD_DIM)
    for ch in range(TM // 128):
        rows = slice(ch * 128, (ch + 1) * 128)
        for pr in range(2):
            cols = slice(pr * LANES, (pr + 1) * LANES)
            vt = v[rows, cols]
            mixed = jnp.where(lo, _dot(ws_ref[2 * pr], vt), _dot(ws_ref[2 * pr + 1], vt))
            p_ref[rows, P_YD + pr * LANES:P_YD + (pr + 1) * LANES] = _bf(
                u[rows, cols] * (mixed + bs_ref[:, cols]))


def _mod_row(i):
    return jnp.minimum(i // TILES_PER_SEQ, BATCH)


def _pos_block(i):
    return jnp.where(i < X_TILES, i % TILES_PER_SEQ, TILES_PER_SEQ + (i - X_TILES) % (CTX_LEN // TM))


def _stage_in(l, xc, mods, w1, tabs, qn, kvn, wuq, wukv, dg, db, ws, bs_exp):
    full = lambda shape: pl.BlockSpec((None,) + shape, lambda i: (l,) + (0,) * len(shape))
    tab = pl.BlockSpec((TM, LANES), lambda i: (_pos_block(i), 0))
    return pl.pallas_call(
        _in_kernel,
        grid=(T // TM,),
        in_specs=[
            pl.BlockSpec((TM, D_MODEL), lambda i: (i, 0)),
            pl.BlockSpec((None, None, 1, 6 * D_MODEL), lambda i: (l, _mod_row(i), 0, 0)),
            full((D_MODEL, W1_W)),
            tab, tab, tab, tab,
            full((1, C_Q_RANK)), full((1, C_KV_RANK)),
            full((C_Q_RANK, 384)), full((C_KV_RANK, 512)),
            full((1, 256)), full((1, 256)),
            full((4, 128, 128)), full((128, 256)),
        ],
        out_specs=pl.BlockSpec((TM, P_W), lambda i: (i, 0)),
        out_shape=jax.ShapeDtypeStruct((T, P_W), jnp.bfloat16),
        compiler_params=pltpu.CompilerParams(vmem_limit_bytes=VMEM_LIMIT),
        name="stage_in",
    )(xc, mods, w1, *tabs, qn, kvn, wuq, wukv, dg, db, ws, bs_exp)


def _mask_half(q_tile, c):
    return jnp.where(_half_mask(c), q_tile, jnp.zeros_like(q_tile))


def _stack_pair(q_tile):
    return jnp.concatenate([_mask_half(q_tile, 0), _mask_half(q_tile, 1)], axis=0)


def _unstack_pair(o):
    n = o.shape[0] // 2
    return jnp.where(_lane_lt(LANES, HEAD_DIM), o[:n], o[n:])


def _sink_column(sink_ref, heads, n):
    return jnp.concatenate([jnp.full((n, 1), sink_ref[h], jnp.float32) for h in heads], axis=0)


def _a_kernel(sink_ref, q_ref, k_ref, v_ref, kc_ref, vc_ref, o_ref):
    g = pl.program_id(1)
    kc, vc = kc_ref[...], vc_ref[...]
    sink = _sink_column(sink_ref, (0, 2, 1, 3), A_BLK)

    def band_start(j):
        return pl.multiple_of(jnp.clip((g * A_SUB + j - 1) * A_BLK, 0, SEQ - A_BAND), A_BLK)

    def scores(j, _):
        rows = slice(j * A_BLK, (j + 1) * A_BLK)
        start = band_start(j)
        qpos = (g * A_SUB + j) * A_BLK + lax.broadcasted_iota(jnp.int32, (A_BLK, A_BAND), 0)
        kpos = start + lax.broadcasted_iota(jnp.int32, (A_BLK, A_BAND), 1)
        valid = jnp.abs(qpos - kpos) <= A_WINDOW
        qs = jnp.concatenate([_stack_pair(q_ref[rows, 0:LANES]), _stack_pair(q_ref[rows, LANES:256])], axis=0)
        s = _dot_nt(qs, k_ref[pl.ds(start, A_BAND), :]).reshape(4, A_BLK, A_BAND)
        s = jnp.where(valid[None], s, NEG_INF).reshape(4 * A_BLK, A_BAND)
        return s, _dot_nt(qs, kc)

    def softmax(j, s):
        return _softmax_num(s, extra=sink)

    def finish(j, num):
        rows = slice(j * A_BLK, (j + 1) * A_BLK)
        o = _pv_normalised(*num, (v_ref[pl.ds(band_start(j), A_BAND), :], vc))
        o_ref[rows, 0:LANES] = _bf(_unstack_pair(o[:2 * A_BLK]))
        o_ref[rows, LANES:256] = _bf(_unstack_pair(o[2 * A_BLK:]))

    _pipelined(A_SUB, (scores, (softmax, finish)))


def _mixer_a(p, sink_l):
    ng = SEQ // (A_BLK * A_SUB)
    cb = NX // CTX_LEN
    return pl.pallas_call(
        _a_kernel,
        grid=(BATCH, ng),
        in_specs=[
            pl.BlockSpec(memory_space=pltpu.SMEM),
            pl.BlockSpec((A_BLK * A_SUB, 256), lambda b, n: (b * ng + n, P_AQ // 256)),
            pl.BlockSpec((SEQ, LANES), lambda b, n: (b, P_AK // LANES)),
            pl.BlockSpec((SEQ, LANES), lambda b, n: (b, P_AV // LANES)),
            pl.BlockSpec((CTX_LEN, LANES), lambda b, n: (cb + b, P_AK // LANES)),
            pl.BlockSpec((CTX_LEN, LANES), lambda b, n: (cb + b, P_AV // LANES)),
        ],
        out_specs=pl.BlockSpec((A_BLK * A_SUB, 256), lambda b, n: (b * ng + n, 0)),
        out_shape=jax.ShapeDtypeStruct((T, 256), jnp.bfloat16),
        compiler_params=pltpu.CompilerParams(vmem_limit_bytes=VMEM_LIMIT),
        name="mixer_a",
    )(sink_l, p, p, p, p, p)


def _b_class(step):
    return jnp.where(step < 2, step, jnp.where(step >= B_STEPS - 2, step - (B_STEPS - B_CLASSES), 2))


def _b_kernel(q_ref, k_ref, v_ref, kc_ref, vc_ref, bias_ref, o_ref):
    g = pl.program_id(1)

    def where(i):
        j, t = i // 2, i % 2
        step = g * B_SUB + j
        w0 = jnp.clip(step * B_QROWS - B_WIN_H // 2, 0, N_ROWS - B_KROWS)
        return (step, slice(j * B_QBLK, (j + 1) * B_QBLK), slice(t * LANES, (t + 1) * LANES), t,
                pl.multiple_of(w0 * GRID_W, GRID_W))

    def scores(i, _):
        step, rows, cols, t, start = where(i)
        qs = _stack_pair(q_ref[rows, cols])
        bias = bias_ref[_b_class(step), 2 * t:2 * t + 2].reshape(2 * B_QBLK, B_KBLK)
        return _dot_nt(qs, k_ref[pl.ds(start, B_KBLK), cols]) + bias, _dot_nt(qs, kc_ref[:, cols])

    def softmax(i, s):
        return _softmax_num(s)

    def finish(i, num):
        _, rows, cols, _, start = where(i)
        o = _pv_normalised(*num, (v_ref[pl.ds(start, B_KBLK), cols], vc_ref[:, cols]))
        o_ref[rows, cols] = _bf(_unstack_pair(o))

    _pipelined(2 * B_SUB, (scores, (softmax, finish)))


def _mixer_b(p, bias_tab):
    ng = B_STEPS // B_SUB
    cb = NX // CTX_LEN
    return pl.pallas_call(
        _b_kernel,
        grid=(BATCH, ng),
        in_specs=[
            pl.BlockSpec((B_QBLK * B_SUB, 256), lambda b, s: (b * ng + s, P_BQ // 256)),
            pl.BlockSpec((SEQ, 256), lambda b, s: (b, P_BK // 256)),
            pl.BlockSpec((SEQ, 256), lambda b, s: (b, P_BV // 256)),
            pl.BlockSpec((CTX_LEN, 256), lambda b, s: (cb + b, P_BK // 256)),
            pl.BlockSpec((CTX_LEN, 256), lambda b, s: (cb + b, P_BV // 256)),
            pl.BlockSpec((B_CLASSES, 4, B_QBLK, B_KBLK), lambda b, s: (0, 0, 0, 0)),
        ],
        out_specs=pl.BlockSpec((B_QBLK * B_SUB, 256), lambda b, s: (b * ng + s, 0)),
        out_shape=jax.ShapeDtypeStruct((T, 256), jnp.bfloat16),
        compiler_params=pltpu.CompilerParams(vmem_limit_bytes=VMEM_LIMIT),
        name="mixer_b",
    )(p, p, p, p, p, bias_tab)


C_SCALE_LOG2E = float((C_NOPE + C_ROPE) ** -0.5 * np.log2(np.e))


def _c_query(qn_tile, qr_tile, h):
    lane = lax.broadcasted_iota(jnp.int32, (1, LANES), 1)
    quarter = (lane >= h * C_ROPE) & (lane < (h + 1) * C_ROPE)
    return jnp.concatenate([_mask_half(qn_tile, h % 2),
                            jnp.where(quarter, qr_tile, jnp.zeros_like(qr_tile))], axis=1)


def _c_pair_queries(qn_tile, qr_tile, t):
    return jnp.concatenate([_c_query(qn_tile, qr_tile, 2 * t), _c_query(qn_tile, qr_tile, 2 * t + 1)], axis=0)


def _c_kernel(qn_ref, qr_ref, kx0_ref, kx1_ref, v_ref, kx0c_ref, kx1c_ref, vc_ref, o_ref):
    kx_refs = ((kx0_ref, kx0c_ref), (kx1_ref, kx1c_ref))

    def where(i):
        j, t = i // 2, i % 2
        return slice(j * C_SUBBLK, (j + 1) * C_SUBBLK), slice(t * LANES, (t + 1) * LANES), t

    def scores(i, _):
        rows, cols, t = where(i)
        qs = _c_pair_queries(qn_ref[rows, cols], qr_ref[rows, :], t)
        return tuple(_dot_nt(qs, r[...]) * C_SCALE_LOG2E for r in kx_refs[t])

    def softmax(i, s):
        return _softmax_num(s, base2=True)

    def finish(i, num):
        rows, cols, _ = where(i)
        o_ref[rows, cols] = _bf(_unstack_pair(_pv_normalised(*num, (v_ref[:, cols], vc_ref[:, cols]))))

    _pipelined(2 * (C_QBLK // C_SUBBLK), (scores, (softmax, finish)))


def _mixer_c(p):
    nq = SEQ // C_QBLK
    cb = NX // CTX_LEN
    return pl.pallas_call(
        _c_kernel,
        grid=(BATCH, nq),
        in_specs=[
            pl.BlockSpec((C_QBLK, 256), lambda b, n: (b * nq + n, P_CQN // 256)),
            pl.BlockSpec((C_QBLK, LANES), lambda b, n: (b * nq + n, P_CQR // LANES)),
            pl.BlockSpec((SEQ, 256), lambda b, n: (b, P_CKX0 // 256)),
            pl.BlockSpec((SEQ, 256), lambda b, n: (b, P_CKX1 // 256)),
            pl.BlockSpec((SEQ, 256), lambda b, n: (b, P_CV // 256)),
            pl.BlockSpec((CTX_LEN, 256), lambda b, n: (cb + b, P_CKX0 // 256)),
            pl.BlockSpec((CTX_LEN, 256), lambda b, n: (cb + b, P_CKX1 // 256)),
            pl.BlockSpec((CTX_LEN, 256), lambda b, n: (cb + b, P_CV // 256)),
        ],
        out_specs=pl.BlockSpec((C_QBLK, 256), lambda b, n: (b * nq + n, 0)),
        out_shape=jax.ShapeDtypeStruct((T, 256), jnp.bfloat16),
        compiler_params=pltpu.CompilerParams(vmem_limit_bytes=VMEM_LIMIT),
        name="mixer_c",
    )(p, p, p, p, p, p, p, p)


def _ctx_kernel(sink_ref, aq_ref, ak_ref, av_ref, bq_ref, bk_ref, bv_ref, cqn_ref, cqr_ref,
                kx0_ref, kx1_ref, cv_ref, ya_in, yb_in, yc_in, ya_ref, yb_ref, yc_ref):
    del ya_in, yb_in, yc_in
    ak, av = ak_ref[...], av_ref[...]
    cqr = cqr_ref[...]
    for t in range(2):
        cols = slice(t * LANES, (t + 1) * LANES)
        qs = _stack_pair(aq_ref[:, cols])
        o = _softmax_pv((_dot_nt(qs, ak),), (av,), extra=_sink_column(sink_ref, (t, t + 2), CTX_LEN))
        ya_ref[:, cols] = _bf(_unstack_pair(o))
        qs = _stack_pair(bq_ref[:, cols])
        yb_ref[:, cols] = _bf(_unstack_pair(_softmax_pv((_dot_nt(qs, bk_ref[:, cols]),), (bv_ref[:, cols],))))
        qs = _c_pair_queries(cqn_ref[:, cols], cqr, t)
        s = _dot_nt(qs, (kx0_ref, kx1_ref)[t][...]) * C_SCALE_LOG2E
        yc_ref[:, cols] = _bf(_unstack_pair(_softmax_pv((s,), (cv_ref[:, cols],), base2=True)))


def _mixers_ctx(p, sink_l, ya, yb, yc):
    cb = NX // CTX_LEN
    blk = lambda w, off: pl.BlockSpec((CTX_LEN, w), lambda b: (cb + b, off // w))
    anyspec = pl.BlockSpec(memory_space=pl.ANY)
    yspec = pl.BlockSpec((CTX_LEN, 256), lambda b: (cb + b, 0))
    yshape = jax.ShapeDtypeStruct((T, 256), jnp.bfloat16)
    return pl.pallas_call(
        _ctx_kernel,
        grid=(BATCH,),
        in_specs=[
            pl.BlockSpec(memory_space=pltpu.SMEM),
            blk(256, P_AQ), blk(LANES, P_AK), blk(LANES, P_AV),
            blk(256, P_BQ), blk(256, P_BK), blk(256, P_BV),
            blk(256, P_CQN), blk(LANES, P_CQR), blk(256, P_CKX0), blk(256, P_CKX1), blk(256, P_CV),
            anyspec, anyspec, anyspec,
        ],
        out_specs=[yspec, yspec, yspec],
        out_shape=[yshape, yshape, yshape],
        input_output_aliases={12: 0, 13: 1, 14: 2},
        compiler_params=pltpu.CompilerParams(vmem_limit_bytes=VMEM_LIMIT),
        name="mixers_ctx",
    )(sink_l, *([p] * 11), ya, yb, yc)


def _post_kernel(x_ref, mod_ref, ya_ref, yb_ref, yc_ref, yd_ref, wo_ref, g1_ref, b1_ref,
                 wgu_ref, wd_ref, g2_ref, b2_ref, o_ref):
    m = lambda j: mod_ref[:, j * D_MODEL:(j + 1) * D_MODEL]
    ycat = jnp.concatenate([ya_ref[...], yb_ref[...], yc_ref[...], yd_ref[...]], axis=1)
    y = _dot(ycat, wo_ref[...])
    x1 = _layer_norm(DN_ALPHA * x_ref[...] + m(2) * y, g1_ref[...], b1_ref[...])
    h = _bf(x1 * (1.0 + m(4)) + m(3))
    acc = jnp.zeros((TM, D_MODEL), jnp.float32)
    for j in range(FF_HIDDEN // FF_CHUNK):
        g = _dot(h, wgu_ref[:, j * FF_CHUNK:(j + 1) * FF_CHUNK])
        u = _dot(h, wgu_ref[:, FF_HIDDEN + j * FF_CHUNK:FF_HIDDEN + (j + 1) * FF_CHUNK])
        a = _bf(g * (1.0 / (1.0 + jnp.exp(-g))) * u)
        acc = acc + _dot(a, wd_ref[j * FF_CHUNK:(j + 1) * FF_CHUNK, :])
    o_ref[...] = _layer_norm(DN_ALPHA * x1 + m(5) * acc, g2_ref[...], b2_ref[...])


def _stage_post(l, n_tiles, xc, mods, ya, yb, yc, p, wo, g1, b1, wgu, wd, g2, b2):
    full = lambda shape: pl.BlockSpec((None,) + shape, lambda i: (l,) + (0,) * len(shape))
    yspec = pl.BlockSpec((TM, 256), lambda i: (i, 0))
    return pl.pallas_call(
        _post_kernel,
        grid=(n_tiles,),
        in_specs=[
            pl.BlockSpec((TM, D_MODEL), lambda i: (i, 0)),
            pl.BlockSpec((None, None, 1, 6 * D_MODEL), lambda i: (l, _mod_row(i), 0, 0)),
            yspec, yspec, yspec,
            pl.BlockSpec((TM, 256), lambda i: (i, P_YD // 256)),
            full((D_MODEL, D_MODEL)), full((1, D_MODEL)), full((1, D_MODEL)),
            full((D_MODEL, 2 * FF_HIDDEN)), full((FF_HIDDEN, D_MODEL)),
            full((1, D_MODEL)), full((1, D_MODEL)),
        ],
        out_specs=pl.BlockSpec((TM, D_MODEL), lambda i: (i, 0)),
        out_shape=jax.ShapeDtypeStruct((n_tiles * TM, D_MODEL), jnp.float32),
        compiler_params=pltpu.CompilerParams(vmem_limit_bytes=VMEM_LIMIT),
        name="stage_post",
    )(xc, mods, ya, yb, yc, p, wo, g1, b1, wgu, wd, g2, b2)


def _rope_tables():
    t = jnp.arange(SEQ)
    pos = jnp.stack([t // GRID_W, t % GRID_W], 0).astype(jnp.float32)

    def table(width):
        half, quarter = width // 2, width // 4
        inv = ROPE_BASE ** (-jnp.arange(0, half, 2, dtype=jnp.float32) / half)
        lane = np.arange(LANES) % width
        axis = lane // half
        k = (lane % half) % quarter
        first = (lane % half) < quarter
        ang = pos[axis, :].T * inv[k][None, :]
        cos = jnp.cos(ang)
        sin = jnp.where(first[None, :], -jnp.sin(ang), jnp.sin(ang))
        ident = (jnp.ones((CTX_LEN, LANES), jnp.float32), jnp.zeros((CTX_LEN, LANES), jnp.float32))
        return jnp.concatenate([cos, ident[0]], 0), jnp.concatenate([sin, ident[1]], 0)

    cosa, sina = table(HEAD_DIM)
    cosc, sinc = table(C_ROPE)
    return cosa, sina, cosc, sinc


def _w1_columns():
    ak, av, bk, bv, cckv, ckr, aq, bq, ccq, du, dv = np.cumsum((0, 128, 128, 256, 256, 128, 32, 256, 256, 256, 256))[:11]
    r = lambda off, n: np.arange(off, off + n)
    aq_cols = np.concatenate([r(aq + 64 * h, 64) for h in (0, 2, 1, 3)])
    return np.concatenate([aq_cols, r(bk, 256), r(bv, 256), r(bq, 256), r(ccq, 256), r(du, 256), r(dv, 256),
                           r(ak, 128), r(av, 128), r(cckv, 128), np.tile(r(ckr, 32), 4)])


def kernel(x, c, ctx, c_ctx, w_mod, b_mod, w_in, a_sink, b_rpb, c_q_norm, c_kv_norm, c_w_uq, c_w_ukv,
           d_ln_g, d_ln_b, d_ws, d_bs, w_out, ln1_g, ln1_b, w_gu, w_down, ln2_g, ln2_b):
    f32 = jnp.float32
    xc = jnp.concatenate([x.reshape(NX, D_MODEL), ctx.reshape(NC, D_MODEL)], axis=0)
    c16 = jnp.concatenate([c, c_ctx[None, :], jnp.zeros((16 - BATCH - 1, D_MODEL), f32)], axis=0)
    mods = _modulation(c16, w_mod, b_mod).reshape(DEPTH, 16, 1, 6 * D_MODEL)

    w1 = _bf(w_in[:, :, _w1_columns()])
    uq_cols = np.concatenate([np.arange(96 * h, 96 * h + 64) for h in range(4)]
                             + [np.arange(96 * h + 64, 96 * h + 96) for h in range(4)])
    ukv_cols = np.concatenate([np.arange(128 * h, 128 * h + 64) for h in range(4)]
                              + [np.arange(128 * h + 64, 128 * h + 128) for h in range(4)])
    wuq = _bf(c_w_uq[:, :, uq_cols])
    wukv = _bf(c_w_ukv[:, :, ukv_cols])
    wo_rows = np.concatenate([np.arange(64 * h, 64 * h + 64) for h in (0, 2, 1, 3)] + [np.arange(256, 1024)])
    wo = _bf(w_out[:, wo_rows, :])
    wgu = _bf(w_gu)
    wd = _bf(w_down)
    ws = _bf(d_ws)
    bs_exp = jnp.repeat(jnp.swapaxes(d_bs, 1, 2), HEAD_DIM, axis=2)
    row = lambda a: a[:, None, :]
    tabs = _rope_tables()

    for l in range(DEPTH):
        last = l == DEPTH - 1
        p = _stage_in(l, xc, mods, w1, tabs, row(c_q_norm), row(c_kv_norm), wuq, wukv,
                      row(d_ln_g), row(d_ln_b), ws, bs_exp)
        ya = _mixer_a(p, a_sink[l])
        yb = _mixer_b(p, _bias_tables(b_rpb[l]))
        yc = _mixer_c(p)
        if not last:
            ya, yb, yc = _mixers_ctx(p, a_sink[l], ya, yb, yc)
        xc = _stage_post(l, X_TILES if last else T // TM, xc, mods, ya, yb, yc, p, wo,
                         row(ln1_g), row(ln1_b), wgu, wd, row(ln2_g), row(ln2_b))
    return xc.reshape(BATCH, SEQ, D_MODEL)
```

```python
import functools

import numpy as np
import jax
import jax.numpy as jnp
from jax import lax
from jax.experimental import pallas as pl
from jax.experimental.pallas import tpu as pltpu

D_MODEL = 1024
BATCH = 8
SEQ = 2048
DEPTH = 4
CTX_LEN = 256
GRID_W = 64
HEAD_DIM = 64
A_WINDOW = 128
B_WIN_H = 8
B_WIN_W = 16
C_Q_RANK = 256
C_KV_RANK = 128
C_NOPE = 64
C_ROPE = 32
FF_HIDDEN = 2816
ROPE_BASE = 10000.0
LN_EPS = 1e-6
NEG_INF = -1e30
DN_ALPHA = (2 * DEPTH) ** 0.25

N_ROWS = SEQ // GRID_W
NX = BATCH * SEQ
NC = BATCH * CTX_LEN
T = NX + NC

LANES = 128
VMEM_LIMIT = 56 * 1024 * 1024

TM = 1024
CHAIN = 512
POST_CHAIN = 512
A_BLK = 128
A_BAND = 3 * A_BLK
B_QROWS = 2
B_QBLK = B_QROWS * GRID_W
B_KROWS = B_WIN_H + B_QROWS
B_KBLK = B_KROWS * GRID_W
B_STEPS = N_ROWS // B_QROWS
B_CLASSES = 5
A_SUB = 8
B_SUB = 8
C_QBLK = 512
C_SUBBLK = 128
FF_CHUNK = 256

P_AQ, P_BK, P_BV, P_BQ, P_CKX0, P_CKX1, P_CV, P_CQN, P_YD = (256 * i for i in range(9))
P_AK, P_AV, P_CQR = 2304, 2432, 2560
P_W = 2688
W1_AQ, W1_BK, W1_BV, W1_BQ, W1_CCQ, W1_DU, W1_DV, W1_AKV, W1_CKV = (256 * i for i in range(9))
W1_W = 2304


def _bf(x):
    return x.astype(jnp.bfloat16)


def _dot(a, b):
    return jnp.dot(a, b, preferred_element_type=jnp.float32)


def _dot_nt(a, b):
    return lax.dot_general(a, b, (((1,), (1,)), ((), ())), preferred_element_type=jnp.float32)


def _lane_lt(n, limit):
    return lax.broadcasted_iota(jnp.int32, (1, n), 1) < limit


def _half_mask(c):
    lane = lax.broadcasted_iota(jnp.int32, (1, LANES), 1)
    return (lane >= c * HEAD_DIM) & (lane < (c + 1) * HEAD_DIM)


def _gelu(x):
    return x * (0.5 * (1.0 + jnp.tanh(np.sqrt(2.0 / np.pi).astype(np.float32) * (x + 0.044715 * (x * x * x)))))


def _layer_norm(x, g, b):
    xc = x - jnp.mean(x, axis=-1, keepdims=True)
    var = jnp.mean(xc * xc, axis=-1, keepdims=True)
    return xc * lax.rsqrt(var + LN_EPS) * g + b


def _rms_norm(x, g):
    return x * lax.rsqrt(jnp.mean(x * x, axis=-1, keepdims=True) + LN_EPS) * g


def _rope(x, cos, sin_signed, half):
    lane = lax.broadcasted_iota(jnp.int32, (1, LANES), 1)
    first = (lane % (2 * half)) < half
    partner = jnp.where(first, pltpu.roll(x, LANES - half, 1), pltpu.roll(x, half, 1))
    return x * cos + partner * sin_signed


def _softmax_num(scores, extra=None, base2=False):
    ex = jnp.exp2 if base2 else jnp.exp
    s = scores[0] if len(scores) == 1 else jnp.concatenate(scores, axis=1)
    m = jnp.max(s, axis=-1, keepdims=True)
    if extra is None:
        return _bf(ex(s - m)), None
    m = jnp.maximum(m, extra)
    return _bf(ex(s - m)), ex(extra - m)


def _pv_normalised(p, extra_num, values):
    v = values[0] if len(values) == 1 else jnp.concatenate(values, axis=0)
    oe = _dot(p, jnp.concatenate([v, jnp.ones_like(v)], axis=1))
    o, l = oe[:, :LANES], oe[:, LANES:]
    return o / (l if extra_num is None else l + extra_num)


def _softmax_pv(scores, values, extra=None, base2=False):
    return _pv_normalised(*_softmax_num(scores, extra, base2), values)


def _pipelined(n, stages):
    stages = [s if callable(s) else (lambda i, c, fs=s: functools.reduce(lambda acc, f: f(i, acc), fs, c))
              for s in stages]
    depth = len(stages)
    carry = [None] * n
    for tick in range(n + depth - 1):
        for k in range(depth):
            i = tick - k
            if 0 <= i < n:
                carry[i] = stages[k](i, carry[i])


MOD_TN = 1536


def _mod_kernel(c_ref, w_ref, b_ref, o_ref):
    c = c_ref[...]
    s = c * (1.0 / (1.0 + jnp.exp(-c)))
    o_ref[...] = _dot(_bf(s), _bf(w_ref[...])) + b_ref[...]


def _modulation(c16, w_mod, b_mod):
    n = 6 * D_MODEL
    return pl.pallas_call(
        _mod_kernel,
        grid=(DEPTH, n // MOD_TN),
        in_specs=[
            pl.BlockSpec((16, D_MODEL), lambda l, j: (0, 0)),
            pl.BlockSpec((None, D_MODEL, MOD_TN), lambda l, j: (l, 0, j)),
            pl.BlockSpec((None, 1, MOD_TN), lambda l, j: (l, 0, j)),
        ],
        out_specs=pl.BlockSpec((None, 16, MOD_TN), lambda l, j: (l, 0, j)),
        out_shape=jax.ShapeDtypeStruct((DEPTH, 16, n), jnp.float32),
        compiler_params=pltpu.CompilerParams(vmem_limit_bytes=VMEM_LIMIT),
        name="modulation",
    )(c16, w_mod, b_mod.reshape(DEPTH, 1, n))


def _b_step_geometry(step):
    r0 = step * B_QROWS
    w0 = min(max(r0 - B_WIN_H // 2, 0), N_ROWS - B_KROWS)
    return r0, w0


_B_CLASS_STEPS = (0, 1, 2, B_STEPS - 2, B_STEPS - 1)


def _bias_kernel(rpb_ref, o_ref):
    n_dr, n_dc = 2 * B_WIN_H - 1, 2 * B_WIN_W - 1
    cq = lax.broadcasted_iota(jnp.int32, (GRID_W, LANES), 0)
    ck = lax.broadcasted_iota(jnp.int32, (GRID_W, LANES), 1) % GRID_W
    c_start = jnp.clip(cq - B_WIN_W // 2, 0, GRID_W - B_WIN_W)
    col_ok = (ck >= c_start) & (ck < c_start + B_WIN_W)
    dc = jnp.clip(ck - cq + (B_WIN_W - 1), 0, n_dc - 1)
    neg = jnp.full((GRID_W, LANES), NEG_INF, jnp.float32)
    lane_lo = lax.broadcasted_iota(jnp.int32, (GRID_W, LANES), 1) < GRID_W
    for h in range(4):
        tz = []
        for dr in range(n_dr):
            t = jnp.zeros((GRID_W, LANES), jnp.float32)
            for d in range(n_dc):
                t = jnp.where(dc == d, rpb_ref[(h * n_dr + dr) * n_dc + d], t)
            tz.append(jnp.where(col_ok, t, neg))
        for ci, step in enumerate(_B_CLASS_STEPS):
            r0, w0 = _b_step_geometry(step)
            for qr in range(B_QROWS):
                r = r0 + qr
                r_start = min(max(r - B_WIN_H // 2, 0), N_ROWS - B_WIN_H)
                for kp in range(B_KROWS // 2):
                    halves = []
                    for kr in (2 * kp, 2 * kp + 1):
                        rk = w0 + kr
                        ok = r_start <= rk < r_start + B_WIN_H
                        halves.append(tz[rk - r + (B_WIN_H - 1)] if ok else neg)
                    o_ref[ci, h, qr * GRID_W:(qr + 1) * GRID_W, kp * LANES:(kp + 1) * LANES] = (
                        jnp.where(lane_lo, halves[0], halves[1]))


def _bias_tables(rpb_l):
    return pl.pallas_call(
        _bias_kernel,
        in_specs=[pl.BlockSpec(memory_space=pltpu.SMEM)],
        out_specs=pl.BlockSpec(memory_space=pltpu.VMEM),
        out_shape=jax.ShapeDtypeStruct((B_CLASSES, 4, B_QBLK, B_KBLK), jnp.float32),
        compiler_params=pltpu.CompilerParams(vmem_limit_bytes=VMEM_LIMIT),
        name="b_bias_tables",
    )(rpb_l.reshape(-1))


def _in_kernel(x_ref, mod_ref, w1_ref, cosa_ref, sina_ref, cosc_ref, sinc_ref, qn_ref, kvn_ref,
               wuq_ref, wukv_ref, dg_ref, db_ref, ws_ref, bs_ref, p_ref):
    shift = mod_ref[:, 0:D_MODEL]
    scale = mod_ref[:, D_MODEL:2 * D_MODEL]
    qscale = HEAD_DIM ** -0.5
    lo = _lane_lt(LANES, HEAD_DIM)

    def projections(c, _):
        rows = slice(c * CHAIN, (c + 1) * CHAIN)
        h = _bf(x_ref[rows, :] * (1.0 + scale) + shift)
        proj = lambda off: _dot(h, w1_ref[:, off:off + 256])
        cosa, sina = cosa_ref[rows, :], sina_ref[rows, :]
        aq = proj(W1_AQ)
        for t in range(2):
            sl = slice(t * LANES, (t + 1) * LANES)
            p_ref[rows, P_AQ + t * LANES:P_AQ + (t + 1) * LANES] = _bf(_rope(aq[:, sl], cosa, sina, 16) * qscale)
        p_ref[rows, P_BK:P_BK + 256] = _bf(proj(W1_BK))
        p_ref[rows, P_BV:P_BV + 256] = _bf(proj(W1_BV))
        p_ref[rows, P_BQ:P_BQ + 256] = _bf(proj(W1_BQ) * qscale)
        akv = proj(W1_AKV)
        p_ref[rows, P_AK:P_AK + LANES] = _bf(_rope(akv[:, :LANES], cosa, sina, 16))
        p_ref[rows, P_AV:P_AV + LANES] = _bf(akv[:, LANES:])
        return proj(W1_CKV), proj(W1_CCQ), proj(W1_DU), proj(W1_DV)

    def mixer_prep(c, carry):
        rows = slice(c * CHAIN, (c + 1) * CHAIN)
        ckv, ccq, du, dv = carry
        cosc, sinc = cosc_ref[rows, :], sinc_ref[rows, :]
        kr4 = _bf(_rope(ckv[:, LANES:], cosc, sinc, 8))
        kv = _dot(_bf(_rms_norm(ckv[:, :LANES], kvn_ref[...])), wukv_ref[...])
        p_ref[rows, P_CKX0:P_CKX0 + LANES] = _bf(kv[:, 0:LANES])
        p_ref[rows, P_CKX0 + LANES:P_CKX0 + 256] = kr4
        p_ref[rows, P_CKX1:P_CKX1 + LANES] = _bf(kv[:, LANES:2 * LANES])
        p_ref[rows, P_CKX1 + LANES:P_CKX1 + 256] = kr4
        p_ref[rows, P_CV:P_CV + 256] = _bf(kv[:, 256:512])
        qq = _dot(_bf(_rms_norm(ccq, qn_ref[...])), wuq_ref[...])
        p_ref[rows, P_CQN:P_CQN + 256] = _bf(qq[:, 0:256])
        p_ref[rows, P_CQR:P_CQR + LANES] = _bf(_rope(qq[:, 256:384], cosc, sinc, 8))
        u = _gelu(du)
        v = _bf(_layer_norm(_gelu(dv), dg_ref[...], db_ref[...]))
        for ch in range(CHAIN // 128):
            sub = slice(ch * 128, (ch + 1) * 128)
            out_rows = slice(c * CHAIN + ch * 128, c * CHAIN + (ch + 1) * 128)
            for pr in range(2):
                cols = slice(pr * LANES, (pr + 1) * LANES)
                vt = v[sub, cols]
                mixed = jnp.where(lo, _dot(ws_ref[2 * pr], vt), _dot(ws_ref[2 * pr + 1], vt))
                p_ref[out_rows, P_YD + pr * LANES:P_YD + (pr + 1) * LANES] = _bf(
                    u[sub, cols] * (mixed + bs_ref[:, cols]))

    _pipelined(TM // CHAIN, (projections, mixer_prep))


def _stage_in(l, x2d, mod_row, tabs, tab_block, mods, w1, qn, kvn, wuq, wukv, dg, db, ws, bs_exp):
    n = x2d.shape[0]
    full = lambda shape: pl.BlockSpec((None,) + shape, lambda i: (l,) + (0,) * len(shape))
    tab = pl.BlockSpec((TM, LANES), lambda i: (tab_block(i), 0))
    return pl.pallas_call(
        _in_kernel,
        grid=(n // TM,),
        in_specs=[
            pl.BlockSpec((TM, D_MODEL), lambda i: (i, 0)),
            pl.BlockSpec((None, None, 1, 6 * D_MODEL), lambda i: (l, mod_row(i), 0, 0)),
            full((D_MODEL, W1_W)),
            tab, tab, tab, tab,
            full((1, C_Q_RANK)), full((1, C_KV_RANK)),
            full((C_Q_RANK, 384)), full((C_KV_RANK, 512)),
            full((1, 256)), full((1, 256)),
            full((4, 128, 128)), full((128, 256)),
        ],
        out_specs=pl.BlockSpec((TM, P_W), lambda i: (i, 0)),
        out_shape=jax.ShapeDtypeStruct((n, P_W), jnp.bfloat16),
        compiler_params=pltpu.CompilerParams(vmem_limit_bytes=VMEM_LIMIT),
        name="stage_in",
    )(x2d, mods, w1, *tabs, qn, kvn, wuq, wukv, dg, db, ws, bs_exp)


def _mask_half(q_tile, c):
    return jnp.where(_half_mask(c), q_tile, jnp.zeros_like(q_tile))


def _stack_pair(q_tile):
    return jnp.concatenate([_mask_half(q_tile, 0), _mask_half(q_tile, 1)], axis=0)


def _unstack_pair(o):
    n = o.shape[0] // 2
    return jnp.where(_lane_lt(LANES, HEAD_DIM), o[:n], o[n:])


def _sink_column(sink_ref, heads, n):
    return jnp.concatenate([jnp.full((n, 1), sink_ref[h], jnp.float32) for h in heads], axis=0)


def _a_kernel(sink_ref, q_ref, k_ref, v_ref, kc_ref, vc_ref, o_ref):
    g = pl.program_id(1)
    kc, vc = kc_ref[...], vc_ref[...]
    sink = _sink_column(sink_ref, (0, 2, 1, 3), A_BLK)

    def band_start(j):
        return pl.multiple_of(jnp.clip((g * A_SUB + j - 1) * A_BLK, 0, SEQ - A_BAND), A_BLK)

    def scores(j, _):
        rows = slice(j * A_BLK, (j + 1) * A_BLK)
        start = band_start(j)
        qpos = (g * A_SUB + j) * A_BLK + lax.broadcasted_iota(jnp.int32, (A_BLK, A_BAND), 0)
        kpos = start + lax.broadcasted_iota(jnp.int32, (A_BLK, A_BAND), 1)
        valid = jnp.abs(qpos - kpos) <= A_WINDOW
        qs = jnp.concatenate([_stack_pair(q_ref[rows, 0:LANES]), _stack_pair(q_ref[rows, LANES:256])], axis=0)
        s = _dot_nt(qs, k_ref[pl.ds(start, A_BAND), :]).reshape(4, A_BLK, A_BAND)
        s = jnp.where(valid[None], s, NEG_INF).reshape(4 * A_BLK, A_BAND)
        return s, _dot_nt(qs, kc)

    def softmax(j, s):
        return _softmax_num(s, extra=sink)

    def finish(j, num):
        rows = slice(j * A_BLK, (j + 1) * A_BLK)
        o = _pv_normalised(*num, (v_ref[pl.ds(band_start(j), A_BAND), :], vc))
        o_ref[rows, 0:LANES] = _bf(_unstack_pair(o[:2 * A_BLK]))
        o_ref[rows, LANES:256] = _bf(_unstack_pair(o[2 * A_BLK:]))

    _pipelined(A_SUB, (scores, (softmax, finish)))


def _mixer_a(p, pc, sink_l):
    ng = SEQ // (A_BLK * A_SUB)
    return pl.pallas_call(
        _a_kernel,
        grid=(BATCH, ng),
        in_specs=[
            pl.BlockSpec(memory_space=pltpu.SMEM),
            pl.BlockSpec((A_BLK * A_SUB, 256), lambda b, n: (b * ng + n, P_AQ // 256)),
            pl.BlockSpec((SEQ, LANES), lambda b, n: (b, P_AK // LANES)),
            pl.BlockSpec((SEQ, LANES), lambda b, n: (b, P_AV // LANES)),
            pl.BlockSpec((CTX_LEN, LANES), lambda b, n: (b, P_AK // LANES)),
            pl.BlockSpec((CTX_LEN, LANES), lambda b, n: (b, P_AV // LANES)),
        ],
        out_specs=pl.BlockSpec((A_BLK * A_SUB, 256), lambda b, n: (b * ng + n, 0)),
        out_shape=jax.ShapeDtypeStruct((NX, 256), jnp.bfloat16),
        compiler_params=pltpu.CompilerParams(vmem_limit_bytes=VMEM_LIMIT),
        name="mixer_a",
    )(sink_l, p, p, p, pc, pc)


def _b_class(step):
    return jnp.where(step < 2, step, jnp.where(step >= B_STEPS - 2, step - (B_STEPS - B_CLASSES), 2))


def _b_kernel(q_ref, k_ref, v_ref, kc_ref, vc_ref, bias_ref, o_ref):
    g = pl.program_id(1)

    def where(i):
        j, t = i // 2, i % 2
        step = g * B_SUB + j
        w0 = jnp.clip(step * B_QROWS - B_WIN_H // 2, 0, N_ROWS - B_KROWS)
        return (step, slice(j * B_QBLK, (j + 1) * B_QBLK), slice(t * LANES, (t + 1) * LANES), t,
                pl.multiple_of(w0 * GRID_W, GRID_W))

    def scores(i, _):
        step, rows, cols, t, start = where(i)
        qs = _stack_pair(q_ref[rows, cols])
        bias = bias_ref[_b_class(step), 2 * t:2 * t + 2].reshape(2 * B_QBLK, B_KBLK)
        return _dot_nt(qs, k_ref[pl.ds(start, B_KBLK), cols]) + bias, _dot_nt(qs, kc_ref[:, cols])

    def softmax(i, s):
        return _softmax_num(s)

    def finish(i, num):
        _, rows, cols, _, start = where(i)
        o = _pv_normalised(*num, (v_ref[pl.ds(start, B_KBLK), cols], vc_ref[:, cols]))
        o_ref[rows, cols] = _bf(_unstack_pair(o))

    _pipelined(2 * B_SUB, (scores, (softmax, finish)))


def _mixer_b(p, pc, bias_tab):
    ng = B_STEPS // B_SUB
    return pl.pallas_call(
        _b_kernel,
        grid=(BATCH, ng),
        in_specs=[
            pl.BlockSpec((B_QBLK * B_SUB, 256), lambda b, s: (b * ng + s, P_BQ // 256)),
            pl.BlockSpec((SEQ, 256), lambda b, s: (b, P_BK // 256)),
            pl.BlockSpec((SEQ, 256), lambda b, s: (b, P_BV // 256)),
            pl.BlockSpec((CTX_LEN, 256), lambda b, s: (b, P_BK // 256)),
            pl.BlockSpec((CTX_LEN, 256), lambda b, s: (b, P_BV // 256)),
            pl.BlockSpec((B_CLASSES, 4, B_QBLK, B_KBLK), lambda b, s: (0, 0, 0, 0)),
        ],
        out_specs=pl.BlockSpec((B_QBLK * B_SUB, 256), lambda b, s: (b * ng + s, 0)),
        out_shape=jax.ShapeDtypeStruct((NX, 256), jnp.bfloat16),
        compiler_params=pltpu.CompilerParams(vmem_limit_bytes=VMEM_LIMIT),
        name="mixer_b",
    )(p, p, p, pc, pc, bias_tab)


C_SCALE_LOG2E = float((C_NOPE + C_ROPE) ** -0.5 * np.log2(np.e))


def _c_query(qn_tile, qr_tile, h):
    lane = lax.broadcasted_iota(jnp.int32, (1, LANES), 1)
    quarter = (lane >= h * C_ROPE) & (lane < (h + 1) * C_ROPE)
    return jnp.concatenate([_mask_half(qn_tile, h % 2),
                            jnp.where(quarter, qr_tile, jnp.zeros_like(qr_tile))], axis=1)


def _c_pair_queries(qn_tile, qr_tile, t):
    return jnp.concatenate([_c_query(qn_tile, qr_tile, 2 * t), _c_query(qn_tile, qr_tile, 2 * t + 1)], axis=0)


def _c_kernel(qn_ref, qr_ref, kx0_ref, kx1_ref, v_ref, kx0c_ref, kx1c_ref, vc_ref, o_ref):
    kx_refs = ((kx0_ref, kx0c_ref), (kx1_ref, kx1c_ref))

    def where(i):
        j, t = i // 2, i % 2
        return slice(j * C_SUBBLK, (j + 1) * C_SUBBLK), slice(t * LANES, (t + 1) * LANES), t

    def scores(i, _):
        rows, cols, t = where(i)
        qs = _c_pair_queries(qn_ref[rows, cols], qr_ref[rows, :], t)
        return tuple(_dot_nt(qs, r[...]) * C_SCALE_LOG2E for r in kx_refs[t])

    def softmax(i, s):
        return _softmax_num(s, base2=True)

    def finish(i, num):
        rows, cols, _ = where(i)
        o_ref[rows, cols] = _bf(_unstack_pair(_pv_normalised(*num, (v_ref[:, cols], vc_ref[:, cols]))))

    _pipelined(2 * (C_QBLK // C_SUBBLK), (scores, (softmax, finish)))


def _mixer_c(p, pc):
    nq = SEQ // C_QBLK
    return pl.pallas_call(
        _c_kernel,
        grid=(BATCH, nq),
        in_specs=[
            pl.BlockSpec((C_QBLK, 256), lambda b, n: (b * nq + n, P_CQN // 256)),
            pl.BlockSpec((C_QBLK, LANES), lambda b, n: (b * nq + n, P_CQR // LANES)),
            pl.BlockSpec((SEQ, 256), lambda b, n: (b, P_CKX0 // 256)),
            pl.BlockSpec((SEQ, 256), lambda b, n: (b, P_CKX1 // 256)),
            pl.BlockSpec((SEQ, 256), lambda b, n: (b, P_CV // 256)),
            pl.BlockSpec((CTX_LEN, 256), lambda b, n: (b, P_CKX0 // 256)),
            pl.BlockSpec((CTX_LEN, 256), lambda b, n: (b, P_CKX1 // 256)),
            pl.BlockSpec((CTX_LEN, 256), lambda b, n: (b, P_CV // 256)),
        ],
        out_specs=pl.BlockSpec((C_QBLK, 256), lambda b, n: (b * nq + n, 0)),
        out_shape=jax.ShapeDtypeStruct((NX, 256), jnp.bfloat16),
        compiler_params=pltpu.CompilerParams(vmem_limit_bytes=VMEM_LIMIT),
        name="mixer_c",
    )(p, p, p, p, p, pc, pc, pc)


def _ctx_kernel(sink_ref, aq_ref, ak_ref, av_ref, bq_ref, bk_ref, bv_ref, cqn_ref, cqr_ref,
                kx0_ref, kx1_ref, cv_ref, ya_ref, yb_ref, yc_ref):
    ak, av = ak_ref[...], av_ref[...]
    cqr = cqr_ref[...]
    for t in range(2):
        cols = slice(t * LANES, (t + 1) * LANES)
        qs = _stack_pair(aq_ref[:, cols])
        o = _softmax_pv((_dot_nt(qs, ak),), (av,), extra=_sink_column(sink_ref, (t, t + 2), CTX_LEN))
        ya_ref[:, cols] = _bf(_unstack_pair(o))
        qs = _stack_pair(bq_ref[:, cols])
        yb_ref[:, cols] = _bf(_unstack_pair(_softmax_pv((_dot_nt(qs, bk_ref[:, cols]),), (bv_ref[:, cols],))))
        qs = _c_pair_queries(cqn_ref[:, cols], cqr, t)
        s = _dot_nt(qs, (kx0_ref, kx1_ref)[t][...]) * C_SCALE_LOG2E
        yc_ref[:, cols] = _bf(_unstack_pair(_softmax_pv((s,), (cv_ref[:, cols],), base2=True)))


def _mixers_ctx(pc, sink_l):
    blk = lambda w, off: pl.BlockSpec((CTX_LEN, w), lambda b: (b, off // w))
    yspec = pl.BlockSpec((CTX_LEN, 256), lambda b: (b, 0))
    yshape = jax.ShapeDtypeStruct((NC, 256), jnp.bfloat16)
    return pl.pallas_call(
        _ctx_kernel,
        grid=(BATCH,),
        in_specs=[
            pl.BlockSpec(memory_space=pltpu.SMEM),
            blk(256, P_AQ), blk(LANES, P_AK), blk(LANES, P_AV),
            blk(256, P_BQ), blk(256, P_BK), blk(256, P_BV),
            blk(256, P_CQN), blk(LANES, P_CQR), blk(256, P_CKX0), blk(256, P_CKX1), blk(256, P_CV),
        ],
        out_specs=[yspec, yspec, yspec],
        out_shape=[yshape, yshape, yshape],
        compiler_params=pltpu.CompilerParams(vmem_limit_bytes=VMEM_LIMIT),
        name="mixers_ctx",
    )(sink_l, *([pc] * 11))


def _post_kernel(x_ref, mod_ref, ya_ref, yb_ref, yc_ref, yd_ref, wo_ref, g1_ref, b1_ref,
                 wgu_ref, wd_ref, g2_ref, b2_ref, o_ref):
    m = lambda j: mod_ref[:, j * D_MODEL:(j + 1) * D_MODEL]

    def mixer_out(c, _):
        rows = slice(c * POST_CHAIN, (c + 1) * POST_CHAIN)
        ycat = jnp.concatenate([ya_ref[rows, :], yb_ref[rows, :], yc_ref[rows, :], yd_ref[rows, :]], axis=1)
        y = _dot(ycat, wo_ref[...])
        x1 = _layer_norm(DN_ALPHA * x_ref[rows, :] + m(2) * y, g1_ref[...], b1_ref[...])
        return x1, _bf(x1 * (1.0 + m(4)) + m(3))

    def ffn(c, carry):
        x1, h = carry
        acc = jnp.zeros((POST_CHAIN, D_MODEL), jnp.float32)
        for j in range(FF_HIDDEN // FF_CHUNK):
            g = _dot(h, wgu_ref[:, j * FF_CHUNK:(j + 1) * FF_CHUNK])
            u = _dot(h, wgu_ref[:, FF_HIDDEN + j * FF_CHUNK:FF_HIDDEN + (j + 1) * FF_CHUNK])
            a = _bf(g * (1.0 / (1.0 + jnp.exp(-g))) * u)
            acc = acc + _dot(a, wd_ref[j * FF_CHUNK:(j + 1) * FF_CHUNK, :])
        o_ref[c * POST_CHAIN:(c + 1) * POST_CHAIN, :] = _layer_norm(
            DN_ALPHA * x1 + m(5) * acc, g2_ref[...], b2_ref[...])

    _pipelined(TM // POST_CHAIN, (mixer_out, ffn))


def _stage_post(l, x2d, mod_row, mods, ya, yb, yc, p, wo, g1, b1, wgu, wd, g2, b2):
    n = x2d.shape[0]
    full = lambda shape: pl.BlockSpec((None,) + shape, lambda i: (l,) + (0,) * len(shape))
    yspec = pl.BlockSpec((TM, 256), lambda i: (i, 0))
    return pl.pallas_call(
        _post_kernel,
        grid=(n // TM,),
        in_specs=[
            pl.BlockSpec((TM, D_MODEL), lambda i: (i, 0)),
            pl.BlockSpec((None, None, 1, 6 * D_MODEL), lambda i: (l, mod_row(i), 0, 0)),
            yspec, yspec, yspec,
            pl.BlockSpec((TM, 256), lambda i: (i, P_YD // 256)),
            full((D_MODEL, D_MODEL)), full((1, D_MODEL)), full((1, D_MODEL)),
            full((D_MODEL, 2 * FF_HIDDEN)), full((FF_HIDDEN, D_MODEL)),
            full((1, D_MODEL)), full((1, D_MODEL)),
        ],
        out_specs=pl.BlockSpec((TM, D_MODEL), lambda i: (i, 0)),
        out_shape=jax.ShapeDtypeStruct((n, D_MODEL), jnp.float32),
        compiler_params=pltpu.CompilerParams(vmem_limit_bytes=VMEM_LIMIT),
        name="stage_post",
    )(x2d, mods, ya, yb, yc, p, wo, g1, b1, wgu, wd, g2, b2)


def _rope_tables():
    t = jnp.arange(SEQ)
    pos = jnp.stack([t // GRID_W, t % GRID_W], 0).astype(jnp.float32)

    def table(width):
        half, quarter = width // 2, width // 4
        inv = ROPE_BASE ** (-jnp.arange(0, half, 2, dtype=jnp.float32) / half)
        lane = np.arange(LANES) % width
        axis = lane // half
        k = (lane % half) % quarter
        first = (lane % half) < quarter
        ang = pos[axis, :].T * inv[k][None, :]
        cos = jnp.cos(ang)
        sin = jnp.where(first[None, :], -jnp.sin(ang), jnp.sin(ang))
        return cos, sin

    return (*table(HEAD_DIM), *table(C_ROPE))


def _take(w, axis, pieces):
    return jnp.concatenate([lax.slice_in_dim(w, off, off + n, axis=axis) for off, n in pieces], axis=axis)


def _w1_pieces():
    ak, av, bk, bv, cckv, ckr, aq, bq, ccq, du, dv = np.cumsum((0, 128, 128, 256, 256, 128, 32, 256, 256, 256, 256))[:11]
    return ([(aq + 64 * h, 64) for h in (0, 2, 1, 3)]
            + [(bk, 256), (bv, 256), (bq, 256), (ccq, 256), (du, 256), (dv, 256), (ak, 128), (av, 128), (cckv, 128)]
            + [(ckr, 32)] * 4)


def kernel(x, c, ctx, c_ctx, w_mod, b_mod, w_in, a_sink, b_rpb, c_q_norm, c_kv_norm, c_w_uq, c_w_ukv,
           d_ln_g, d_ln_b, d_ws, d_bs, w_out, ln1_g, ln1_b, w_gu, w_down, ln2_g, ln2_b):
    f32 = jnp.float32
    xl = x.reshape(NX, D_MODEL)
    xc = ctx.reshape(NC, D_MODEL)
    c16 = jnp.concatenate([c, c_ctx[None, :], jnp.zeros((16 - BATCH - 1, D_MODEL), f32)], axis=0)
    mods = _modulation(c16, w_mod, b_mod).reshape(DEPTH, 16, 1, 6 * D_MODEL)

    w1 = _bf(_take(w_in, 2, _w1_pieces()))
    wuq = _bf(_take(c_w_uq, 2, [(96 * h, 64) for h in range(4)] + [(96 * h + 64, 32) for h in range(4)]))
    wukv = _bf(_take(c_w_ukv, 2, [(128 * h, 64) for h in range(4)] + [(128 * h + 64, 64) for h in range(4)]))
    wo = _bf(_take(w_out, 1, [(64 * h, 64) for h in (0, 2, 1, 3)] + [(256, 768)]))
    wgu = _bf(w_gu)
    wd = _bf(w_down)
    ws = _bf(d_ws)
    bs_exp = jnp.repeat(jnp.swapaxes(d_bs, 1, 2), HEAD_DIM, axis=2)
    row = lambda a: a[:, None, :]
    tabs = _rope_tables()
    tabs_ctx = (jnp.ones((TM, LANES), f32), jnp.zeros((TM, LANES), f32)) * 2
    lat_mod = lambda i: i // (SEQ // TM)
    lat_tab = lambda i: i % (SEQ // TM)
    ctx_mod = lambda i: BATCH
    ctx_tab = lambda i: 0

    for l in range(DEPTH):
        last = l == DEPTH - 1
        in_params = (mods, w1, row(c_q_norm), row(c_kv_norm), wuq, wukv, row(d_ln_g), row(d_ln_b), ws, bs_exp)
        post_params = (wo, row(ln1_g), row(ln1_b), wgu, wd, row(ln2_g), row(ln2_b))
        p = _stage_in(l, xl, lat_mod, tabs, lat_tab, *in_params)
        pc = _stage_in(l, xc, ctx_mod, tabs_ctx, ctx_tab, *in_params)
        ya = _mixer_a(p, pc, a_sink[l])
        yb = _mixer_b(p, pc, _bias_tables(b_rpb[l]))
        yc = _mixer_c(p, pc)
        xl = _stage_post(l, xl, lat_mod, mods, ya, yb, yc, p, *post_params)
        if not last:
            xc = _stage_post(l, xc, ctx_mod, mods, *_mixers_ctx(pc, a_sink[l]), pc, *post_params)
    return xl.reshape(BATCH, SEQ, D_MODEL)
```

```python
import functools

import numpy as np
import jax
import jax.numpy as jnp
from jax import lax
from jax.experimental import pallas as pl
from jax.experimental.pallas import tpu as pltpu

D_MODEL = 1024
BATCH = 8
SEQ = 2048
DEPTH = 4
CTX_LEN = 256
GRID_W = 64
HEAD_DIM = 64
A_WINDOW = 128
B_WIN_H = 8
B_WIN_W = 16
C_Q_RANK = 256
C_KV_RANK = 128
C_NOPE = 64
C_ROPE = 32
FF_HIDDEN = 2816
ROPE_BASE = 10000.0
LN_EPS = 1e-6
NEG_INF = -1e30
DN_ALPHA = (2 * DEPTH) ** 0.25

N_ROWS = SEQ // GRID_W
NX = BATCH * SEQ
NC = BATCH * CTX_LEN

LANES = 128
VMEM_LIMIT = 56 * 1024 * 1024

TM = 1024
CHAIN = 512
POST_CHAIN = 512
A_BLK = 128
A_BAND = 3 * A_BLK
B_QROWS = 2
B_QBLK = B_QROWS * GRID_W
B_KROWS = B_WIN_H + B_QROWS
B_KBLK = B_KROWS * GRID_W
B_STEPS = N_ROWS // B_QROWS
B_CLASSES = 5
A_SUB = 8
B_SUB = 8
C_QBLK = 512
C_SUBBLK = 128
FF_CHUNK = 256

P_AQ, P_BK, P_BV, P_BQ, P_CKX0, P_CKX1, P_CV, P_CQN, P_YD = (256 * i for i in range(9))
P_AK, P_AV, P_CQR = 2304, 2432, 2560
P_W = 2688
W1_AQ, W1_BK, W1_BV, W1_BQ, W1_CCQ, W1_DU, W1_DV, W1_AKV, W1_CKV = (256 * i for i in range(9))
W1_W = 2304


def _bf(x):
    return x.astype(jnp.bfloat16)


def _dot(a, b):
    return jnp.dot(a, b, preferred_element_type=jnp.float32)


def _dot_nt(a, b):
    return lax.dot_general(a, b, (((1,), (1,)), ((), ())), preferred_element_type=jnp.float32)


def _lane_lt(n, limit):
    return lax.broadcasted_iota(jnp.int32, (1, n), 1) < limit


def _half_mask(c):
    lane = lax.broadcasted_iota(jnp.int32, (1, LANES), 1)
    return (lane >= c * HEAD_DIM) & (lane < (c + 1) * HEAD_DIM)


def _gelu(x):
    return x * (0.5 * (1.0 + jnp.tanh(np.sqrt(2.0 / np.pi).astype(np.float32) * (x + 0.044715 * (x * x * x)))))


def _layer_norm(x, g, b):
    xc = x - jnp.mean(x, axis=-1, keepdims=True)
    var = jnp.mean(xc * xc, axis=-1, keepdims=True)
    return xc * lax.rsqrt(var + LN_EPS) * g + b


def _rms_norm(x, g):
    return x * lax.rsqrt(jnp.mean(x * x, axis=-1, keepdims=True) + LN_EPS) * g


def _rope(x, cos, sin_signed, half):
    lane = lax.broadcasted_iota(jnp.int32, (1, LANES), 1)
    first = (lane % (2 * half)) < half
    partner = jnp.where(first, pltpu.roll(x, LANES - half, 1), pltpu.roll(x, half, 1))
    return x * cos + partner * sin_signed


def _softmax_num(scores, extra=None, base2=False):
    ex = jnp.exp2 if base2 else jnp.exp
    s = scores[0] if len(scores) == 1 else jnp.concatenate(scores, axis=1)
    m = jnp.max(s, axis=-1, keepdims=True)
    if extra is None:
        return _bf(ex(s - m)), None
    m = jnp.maximum(m, extra)
    return _bf(ex(s - m)), ex(extra - m)


def _pv_normalised(p, extra_num, values):
    v = values[0] if len(values) == 1 else jnp.concatenate(values, axis=0)
    oe = _dot(p, jnp.concatenate([v, jnp.ones_like(v)], axis=1))
    o, l = oe[:, :LANES], oe[:, LANES:]
    return o / (l if extra_num is None else l + extra_num)


def _softmax_pv(scores, values, extra=None, base2=False):
    return _pv_normalised(*_softmax_num(scores, extra, base2), values)


def _pipelined(n, stages):
    stages = [s if callable(s) else (lambda i, c, fs=s: functools.reduce(lambda acc, f: f(i, acc), fs, c))
              for s in stages]
    depth = len(stages)
    carry = [None] * n
    for tick in range(n + depth - 1):
        for k in range(depth):
            i = tick - k
            if 0 <= i < n:
                carry[i] = stages[k](i, carry[i])


MOD_TN = 1536


def _mod_kernel(c_ref, w_ref, b_ref, o_ref):
    c = c_ref[...]
    s = c * (1.0 / (1.0 + jnp.exp(-c)))
    o_ref[...] = _dot(_bf(s), _bf(w_ref[...])) + b_ref[...]


def _modulation(c16, w_mod, b_mod):
    n = 6 * D_MODEL
    return pl.pallas_call(
        _mod_kernel,
        grid=(DEPTH, n // MOD_TN),
        in_specs=[
            pl.BlockSpec((16, D_MODEL), lambda l, j: (0, 0)),
            pl.BlockSpec((None, D_MODEL, MOD_TN), lambda l, j: (l, 0, j)),
            pl.BlockSpec((None, 1, MOD_TN), lambda l, j: (l, 0, j)),
        ],
        out_specs=pl.BlockSpec((None, 16, MOD_TN), lambda l, j: (l, 0, j)),
        out_shape=jax.ShapeDtypeStruct((DEPTH, 16, n), jnp.float32),
        compiler_params=pltpu.CompilerParams(vmem_limit_bytes=VMEM_LIMIT),
        name="modulation",
    )(c16, w_mod, b_mod.reshape(DEPTH, 1, n))


def _take(w, axis, pieces):
    return jnp.concatenate([lax.slice_in_dim(w, off, off + n, axis=axis) for off, n in pieces], axis=axis)


def _w1_pieces():
    ak, av, bk, bv, cckv, ckr, aq, bq, ccq, du, dv = np.cumsum((0, 128, 128, 256, 256, 128, 32, 256, 256, 256, 256))[:11]
    return ([(aq + 64 * h, 64) for h in (0, 2, 1, 3)]
            + [(bk, 256), (bv, 256), (bq, 256), (ccq, 256), (du, 256), (dv, 256), (ak, 128), (av, 128), (cckv, 128)]
            + [(ckr, 32)] * 4)


_UQ_PIECES = [(96 * h, 64) for h in range(4)] + [(96 * h + 64, 32) for h in range(4)]
_UKV_PIECES = [(128 * h, 64) for h in range(4)] + [(128 * h + 64, 64) for h in range(4)]
PREP_ROWS = 256
_WO_PIECES_BLOCK0 = [(64 * h, 64) for h in (0, 2, 1, 3)]


def _prep_kernel(win_ref, wuq_ref, wukv_ref, wout_ref, w1_ref, uq_ref, ukv_ref, wo_ref):
    w1_ref[...] = _bf(_take(win_ref[...], 1, _w1_pieces()))
    wout = wout_ref[...]
    wo_ref[...] = _bf(jnp.where(pl.program_id(1) == 0, _take(wout, 0, _WO_PIECES_BLOCK0), wout))

    @pl.when(pl.program_id(1) == 0)
    def _():
        uq_ref[...] = _bf(_take(wuq_ref[...], 1, _UQ_PIECES))
        ukv_ref[...] = _bf(_take(wukv_ref[...], 1, _UKV_PIECES))


def _prepare_weights(w_in, c_w_uq, c_w_ukv, w_out):
    nr = D_MODEL // PREP_ROWS
    assert PREP_ROWS == 4 * HEAD_DIM
    return pl.pallas_call(
        _prep_kernel,
        grid=(DEPTH, nr),
        in_specs=[
            pl.BlockSpec((None, PREP_ROWS, w_in.shape[2]), lambda l, r: (l, r, 0)),
            pl.BlockSpec((None,) + c_w_uq.shape[1:], lambda l, r: (l, 0, 0)),
            pl.BlockSpec((None,) + c_w_ukv.shape[1:], lambda l, r: (l, 0, 0)),
            pl.BlockSpec((None, PREP_ROWS, D_MODEL), lambda l, r: (l, r, 0)),
        ],
        out_specs=[
            pl.BlockSpec((None, PREP_ROWS, W1_W), lambda l, r: (l, r, 0)),
            pl.BlockSpec((None,) + c_w_uq.shape[1:], lambda l, r: (l, 0, 0)),
            pl.BlockSpec((None,) + c_w_ukv.shape[1:], lambda l, r: (l, 0, 0)),
            pl.BlockSpec((None, PREP_ROWS, D_MODEL), lambda l, r: (l, r, 0)),
        ],
        out_shape=[
            jax.ShapeDtypeStruct((DEPTH, D_MODEL, W1_W), jnp.bfloat16),
            jax.ShapeDtypeStruct(c_w_uq.shape, jnp.bfloat16),
            jax.ShapeDtypeStruct(c_w_ukv.shape, jnp.bfloat16),
            jax.ShapeDtypeStruct(w_out.shape, jnp.bfloat16),
        ],
        compiler_params=pltpu.CompilerParams(vmem_limit_bytes=VMEM_LIMIT),
        name="prepare_weights",
    )(w_in, c_w_uq, c_w_ukv, w_out)


def _b_step_geometry(step):
    r0 = step * B_QROWS
    w0 = min(max(r0 - B_WIN_H // 2, 0), N_ROWS - B_KROWS)
    return r0, w0


_B_CLASS_STEPS = (0, 1, 2, B_STEPS - 2, B_STEPS - 1)


def _bias_kernel(rpb_ref, o_ref):
    n_dr, n_dc = 2 * B_WIN_H - 1, 2 * B_WIN_W - 1
    cq = lax.broadcasted_iota(jnp.int32, (GRID_W, LANES), 0)
    ck = lax.broadcasted_iota(jnp.int32, (GRID_W, LANES), 1) % GRID_W
    c_start = jnp.clip(cq - B_WIN_W // 2, 0, GRID_W - B_WIN_W)
    col_ok = (ck >= c_start) & (ck < c_start + B_WIN_W)
    dc = jnp.clip(ck - cq + (B_WIN_W - 1), 0, n_dc - 1)
    neg = jnp.full((GRID_W, LANES), NEG_INF, jnp.float32)
    lane_lo = lax.broadcasted_iota(jnp.int32, (GRID_W, LANES), 1) < GRID_W
    for h in range(4):
        tz = []
        for dr in range(n_dr):
            t = jnp.zeros((GRID_W, LANES), jnp.float32)
            for d in range(n_dc):
                t = jnp.where(dc == d, rpb_ref[(h * n_dr + dr) * n_dc + d], t)
            tz.append(jnp.where(col_ok, t, neg))
        for ci, step in enumerate(_B_CLASS_STEPS):
            r0, w0 = _b_step_geometry(step)
            for qr in range(B_QROWS):
                r = r0 + qr
                r_start = min(max(r - B_WIN_H // 2, 0), N_ROWS - B_WIN_H)
                for kp in range(B_KROWS // 2):
                    halves = []
                    for kr in (2 * kp, 2 * kp + 1):
                        rk = w0 + kr
                        ok = r_start <= rk < r_start + B_WIN_H
                        halves.append(tz[rk - r + (B_WIN_H - 1)] if ok else neg)
                    o_ref[ci, h, qr * GRID_W:(qr + 1) * GRID_W, kp * LANES:(kp + 1) * LANES] = (
                        jnp.where(lane_lo, halves[0], halves[1]))


def _bias_tables(rpb_l):
    return pl.pallas_call(
        _bias_kernel,
        in_specs=[pl.BlockSpec(memory_space=pltpu.SMEM)],
        out_specs=pl.BlockSpec(memory_space=pltpu.VMEM),
        out_shape=jax.ShapeDtypeStruct((B_CLASSES, 4, B_QBLK, B_KBLK), jnp.float32),
        compiler_params=pltpu.CompilerParams(vmem_limit_bytes=VMEM_LIMIT),
        name="b_bias_tables",
    )(rpb_l.reshape(-1))


def _in_kernel(x_ref, mod_ref, w1_ref, cosa_ref, sina_ref, cosc_ref, sinc_ref, qn_ref, kvn_ref,
               wuq_ref, wukv_ref, dg_ref, db_ref, ws_ref, bs_ref, *rest, cast_ffn):
    if cast_ffn:
        wgu_ref, wd_ref, p_ref, wgu_out_ref, wd_out_ref = rest
        wgu_out_ref[...] = _bf(wgu_ref[...])
        wd_out_ref[...] = _bf(wd_ref[...])
    else:
        (p_ref,) = rest
    shift = mod_ref[:, 0:D_MODEL]
    scale = mod_ref[:, D_MODEL:2 * D_MODEL]
    qscale = HEAD_DIM ** -0.5
    lo = _lane_lt(LANES, HEAD_DIM)

    def projections(c, _):
        rows = slice(c * CHAIN, (c + 1) * CHAIN)
        h = _bf(x_ref[rows, :] * (1.0 + scale) + shift)
        proj = lambda off: _dot(h, w1_ref[:, off:off + 256])
        cosa, sina = cosa_ref[rows, :], sina_ref[rows, :]
        aq = proj(W1_AQ)
        for t in range(2):
            sl = slice(t * LANES, (t + 1) * LANES)
            p_ref[rows, P_AQ + t * LANES:P_AQ + (t + 1) * LANES] = _bf(_rope(aq[:, sl], cosa, sina, 16) * qscale)
        p_ref[rows, P_BK:P_BK + 256] = _bf(proj(W1_BK))
        p_ref[rows, P_BV:P_BV + 256] = _bf(proj(W1_BV))
        p_ref[rows, P_BQ:P_BQ + 256] = _bf(proj(W1_BQ) * qscale)
        akv = proj(W1_AKV)
        p_ref[rows, P_AK:P_AK + LANES] = _bf(_rope(akv[:, :LANES], cosa, sina, 16))
        p_ref[rows, P_AV:P_AV + LANES] = _bf(akv[:, LANES:])
        return proj(W1_CKV), proj(W1_CCQ), proj(W1_DU), proj(W1_DV)

    def mixer_prep(c, carry):
        rows = slice(c * CHAIN, (c + 1) * CHAIN)
        ckv, ccq, du, dv = carry
        cosc, sinc = cosc_ref[rows, :], sinc_ref[rows, :]
        kr4 = _bf(_rope(ckv[:, LANES:], cosc, sinc, 8))
        kv = _dot(_bf(_rms_norm(ckv[:, :LANES], kvn_ref[...])), wukv_ref[...])
        p_ref[rows, P_CKX0:P_CKX0 + LANES] = _bf(kv[:, 0:LANES])
        p_ref[rows, P_CKX0 + LANES:P_CKX0 + 256] = kr4
        p_ref[rows, P_CKX1:P_CKX1 + LANES] = _bf(kv[:, LANES:2 * LANES])
        p_ref[rows, P_CKX1 + LANES:P_CKX1 + 256] = kr4
        p_ref[rows, P_CV:P_CV + 256] = _bf(kv[:, 256:512])
        qq = _dot(_bf(_rms_norm(ccq, qn_ref[...])), wuq_ref[...])
        p_ref[rows, P_CQN:P_CQN + 256] = _bf(qq[:, 0:256])
        p_ref[rows, P_CQR:P_CQR + LANES] = _bf(_rope(qq[:, 256:384], cosc, sinc, 8))
        u = _gelu(du)
        v = _bf(_layer_norm(_gelu(dv), dg_ref[...], db_ref[...]))
        for ch in range(CHAIN // 128):
            sub = slice(ch * 128, (ch + 1) * 128)
            out_rows = slice(c * CHAIN + ch * 128, c * CHAIN + (ch + 1) * 128)
            for pr in range(2):
                cols = slice(pr * LANES, (pr + 1) * LANES)
                vt = v[sub, cols]
                mixed = jnp.where(lo, _dot(ws_ref[2 * pr], vt), _dot(ws_ref[2 * pr + 1], vt))
                p_ref[out_rows, P_YD + pr * LANES:P_YD + (pr + 1) * LANES] = _bf(
                    u[sub, cols] * (mixed + bs_ref[:, cols]))

    _pipelined(TM // CHAIN, (projections, mixer_prep))


def _stage_in(l, x2d, mod_row, tabs, tab_block, mods, w1, qn, kvn, wuq, wukv, dg, db, ws, bs_exp, ffn_f32=None):
    n = x2d.shape[0]
    steps = n // TM
    full = lambda shape: pl.BlockSpec((None,) + shape, lambda i: (l,) + (0,) * len(shape))
    tab = pl.BlockSpec((TM, LANES), lambda i: (tab_block(i), 0))
    in_specs = [
        pl.BlockSpec((TM, D_MODEL), lambda i: (i, 0)),
        pl.BlockSpec((None, None, 1, 6 * D_MODEL), lambda i: (l, mod_row(i), 0, 0)),
        full((D_MODEL, W1_W)),
        tab, tab, tab, tab,
        full((1, C_Q_RANK)), full((1, C_KV_RANK)),
        full((C_Q_RANK, 384)), full((C_KV_RANK, 512)),
        full((1, 256)), full((1, 256)),
        full((4, 128, 128)), full((128, 256)),
    ]
    out_specs = [pl.BlockSpec((TM, P_W), lambda i: (i, 0))]
    out_shape = [jax.ShapeDtypeStruct((n, P_W), jnp.bfloat16)]
    operands = [x2d, mods, w1, *tabs, qn, kvn, wuq, wukv, dg, db, ws, bs_exp]
    if ffn_f32 is not None:
        for w in ffn_f32:
            rows, cols = w.shape[1] // steps, w.shape[2]
            in_specs.append(pl.BlockSpec((None, rows, cols), lambda i: (l, i, 0)))
            out_specs.append(pl.BlockSpec((rows, cols), lambda i: (i, 0)))
            out_shape.append(jax.ShapeDtypeStruct(w.shape[1:], jnp.bfloat16))
            operands.append(w)
    return pl.pallas_call(
        functools.partial(_in_kernel, cast_ffn=ffn_f32 is not None),
        grid=(steps,),
        in_specs=in_specs,
        out_specs=out_specs,
        out_shape=out_shape,
        compiler_params=pltpu.CompilerParams(vmem_limit_bytes=VMEM_LIMIT),
        name="stage_in",
    )(*operands)


def _mask_half(q_tile, c):
    return jnp.where(_half_mask(c), q_tile, jnp.zeros_like(q_tile))


def _stack_pair(q_tile):
    return jnp.concatenate([_mask_half(q_tile, 0), _mask_half(q_tile, 1)], axis=0)


def _unstack_pair(o):
    n = o.shape[0] // 2
    return jnp.where(_lane_lt(LANES, HEAD_DIM), o[:n], o[n:])


def _sink_column(sink_ref, heads, n):
    return jnp.concatenate([jnp.full((n, 1), sink_ref[h], jnp.float32) for h in heads], axis=0)


def _a_kernel(sink_ref, q_ref, k_ref, v_ref, kc_ref, vc_ref, o_ref):
    g = pl.program_id(1)
    kc, vc = kc_ref[...], vc_ref[...]
    sink = _sink_column(sink_ref, (0, 2, 1, 3), A_BLK)

    def band_start(j):
        return pl.multiple_of(jnp.clip((g * A_SUB + j - 1) * A_BLK, 0, SEQ - A_BAND), A_BLK)

    def scores(j, _):
        rows = slice(j * A_BLK, (j + 1) * A_BLK)
        start = band_start(j)
        qpos = (g * A_SUB + j) * A_BLK + lax.broadcasted_iota(jnp.int32, (A_BLK, A_BAND), 0)
        kpos = start + lax.broadcasted_iota(jnp.int32, (A_BLK, A_BAND), 1)
        valid = jnp.abs(qpos - kpos) <= A_WINDOW
        qs = jnp.concatenate([_stack_pair(q_ref[rows, 0:LANES]), _stack_pair(q_ref[rows, LANES:256])], axis=0)
        s = _dot_nt(qs, k_ref[pl.ds(start, A_BAND), :]).reshape(4, A_BLK, A_BAND)
        s = jnp.where(valid[None], s, NEG_INF).reshape(4 * A_BLK, A_BAND)
        return s, _dot_nt(qs, kc)

    def softmax(j, s):
        return _softmax_num(s, extra=sink)

    def finish(j, num):
        rows = slice(j * A_BLK, (j + 1) * A_BLK)
        o = _pv_normalised(*num, (v_ref[pl.ds(band_start(j), A_BAND), :], vc))
        o_ref[rows, 0:LANES] = _bf(_unstack_pair(o[:2 * A_BLK]))
        o_ref[rows, LANES:256] = _bf(_unstack_pair(o[2 * A_BLK:]))

    _pipelined(A_SUB, (scores, (softmax, finish)))


def _mixer_a(p, pc, sink_l):
    ng = SEQ // (A_BLK * A_SUB)
    return pl.pallas_call(
        _a_kernel,
        grid=(BATCH, ng),
        in_specs=[
            pl.BlockSpec(memory_space=pltpu.SMEM),
            pl.BlockSpec((A_BLK * A_SUB, 256), lambda b, n: (b * ng + n, P_AQ // 256)),
            pl.BlockSpec((SEQ, LANES), lambda b, n: (b, P_AK // LANES)),
            pl.BlockSpec((SEQ, LANES), lambda b, n: (b, P_AV // LANES)),
            pl.BlockSpec((CTX_LEN, LANES), lambda b, n: (b, P_AK // LANES)),
            pl.BlockSpec((CTX_LEN, LANES), lambda b, n: (b, P_AV // LANES)),
        ],
        out_specs=pl.BlockSpec((A_BLK * A_SUB, 256), lambda b, n: (b * ng + n, 0)),
        out_shape=jax.ShapeDtypeStruct((NX, 256), jnp.bfloat16),
        compiler_params=pltpu.CompilerParams(vmem_limit_bytes=VMEM_LIMIT),
        name="mixer_a",
    )(sink_l, p, p, p, pc, pc)


def _b_class(step):
    return jnp.where(step < 2, step, jnp.where(step >= B_STEPS - 2, step - (B_STEPS - B_CLASSES), 2))


def _b_kernel(q_ref, k_ref, v_ref, kc_ref, vc_ref, bias_ref, o_ref):
    g = pl.program_id(1)

    def where(i):
        j, t = i // 2, i % 2
        step = g * B_SUB + j
        w0 = jnp.clip(step * B_QROWS - B_WIN_H // 2, 0, N_ROWS - B_KROWS)
        return (step, slice(j * B_QBLK, (j + 1) * B_QBLK), slice(t * LANES, (t + 1) * LANES), t,
                pl.multiple_of(w0 * GRID_W, GRID_W))

    def scores(i, _):
        step, rows, cols, t, start = where(i)
        qs = _stack_pair(q_ref[rows, cols])
        bias = bias_ref[_b_class(step), 2 * t:2 * t + 2].reshape(2 * B_QBLK, B_KBLK)
        return _dot_nt(qs, k_ref[pl.ds(start, B_KBLK), cols]) + bias, _dot_nt(qs, kc_ref[:, cols])

    def softmax(i, s):
        return _softmax_num(s)

    def finish(i, num):
        _, rows, cols, _, start = where(i)
        o = _pv_normalised(*num, (v_ref[pl.ds(start, B_KBLK), cols], vc_ref[:, cols]))
        o_ref[rows, cols] = _bf(_unstack_pair(o))

    _pipelined(2 * B_SUB, (scores, (softmax, finish)))


def _mixer_b(p, pc, bias_tab):
    ng = B_STEPS // B_SUB
    return pl.pallas_call(
        _b_kernel,
        grid=(BATCH, ng),
        in_specs=[
            pl.BlockSpec((B_QBLK * B_SUB, 256), lambda b, s: (b * ng + s, P_BQ // 256)),
            pl.BlockSpec((SEQ, 256), lambda b, s: (b, P_BK // 256)),
            pl.BlockSpec((SEQ, 256), lambda b, s: (b, P_BV // 256)),
            pl.BlockSpec((CTX_LEN, 256), lambda b, s: (b, P_BK // 256)),
            pl.BlockSpec((CTX_LEN, 256), lambda b, s: (b, P_BV // 256)),
            pl.BlockSpec((B_CLASSES, 4, B_QBLK, B_KBLK), lambda b, s: (0, 0, 0, 0)),
        ],
        out_specs=pl.BlockSpec((B_QBLK * B_SUB, 256), lambda b, s: (b * ng + s, 0)),
        out_shape=jax.ShapeDtypeStruct((NX, 256), jnp.bfloat16),
        compiler_params=pltpu.CompilerParams(vmem_limit_bytes=VMEM_LIMIT),
        name="mixer_b",
    )(p, p, p, pc, pc, bias_tab)


C_SCALE_LOG2E = float((C_NOPE + C_ROPE) ** -0.5 * np.log2(np.e))


def _c_query(qn_tile, qr_tile, h):
    lane = lax.broadcasted_iota(jnp.int32, (1, LANES), 1)
    quarter = (lane >= h * C_ROPE) & (lane < (h + 1) * C_ROPE)
    return jnp.concatenate([_mask_half(qn_tile, h % 2),
                            jnp.where(quarter, qr_tile, jnp.zeros_like(qr_tile))], axis=1)


def _c_pair_queries(qn_tile, qr_tile, t):
    return jnp.concatenate([_c_query(qn_tile, qr_tile, 2 * t), _c_query(qn_tile, qr_tile, 2 * t + 1)], axis=0)


def _c_kernel(qn_ref, qr_ref, kx0_ref, kx1_ref, v_ref, kx0c_ref, kx1c_ref, vc_ref, o_ref):
    kx_refs = ((kx0_ref, kx0c_ref), (kx1_ref, kx1c_ref))

    def where(i):
        j, t = i // 2, i % 2
        return slice(j * C_SUBBLK, (j + 1) * C_SUBBLK), slice(t * LANES, (t + 1) * LANES), t

    def scores(i, _):
        rows, cols, t = where(i)
        qs = _c_pair_queries(qn_ref[rows, cols], qr_ref[rows, :], t)
        return tuple(_dot_nt(qs, r[...]) * C_SCALE_LOG2E for r in kx_refs[t])

    def softmax(i, s):
        return _softmax_num(s, base2=True)

    def finish(i, num):
        rows, cols, _ = where(i)
        o_ref[rows, cols] = _bf(_unstack_pair(_pv_normalised(*num, (v_ref[:, cols], vc_ref[:, cols]))))

    _pipelined(2 * (C_QBLK // C_SUBBLK), (scores, (softmax, finish)))


def _mixer_c(p, pc):
    nq = SEQ // C_QBLK
    return pl.pallas_call(
        _c_kernel,
        grid=(BATCH, nq),
        in_specs=[
            pl.BlockSpec((C_QBLK, 256), lambda b, n: (b * nq + n, P_CQN // 256)),
            pl.BlockSpec((C_QBLK, LANES), lambda b, n: (b * nq + n, P_CQR // LANES)),
            pl.BlockSpec((SEQ, 256), lambda b, n: (b, P_CKX0 // 256)),
            pl.BlockSpec((SEQ, 256), lambda b, n: (b, P_CKX1 // 256)),
            pl.BlockSpec((SEQ, 256), lambda b, n: (b, P_CV // 256)),
            pl.BlockSpec((CTX_LEN, 256), lambda b, n: (b, P_CKX0 // 256)),
            pl.BlockSpec((CTX_LEN, 256), lambda b, n: (b, P_CKX1 // 256)),
            pl.BlockSpec((CTX_LEN, 256), lambda b, n: (b, P_CV // 256)),
        ],
        out_specs=pl.BlockSpec((C_QBLK, 256), lambda b, n: (b * nq + n, 0)),
        out_shape=jax.ShapeDtypeStruct((NX, 256), jnp.bfloat16),
        compiler_params=pltpu.CompilerParams(vmem_limit_bytes=VMEM_LIMIT),
        name="mixer_c",
    )(p, p, p, p, p, pc, pc, pc)


def _ctx_kernel(sink_ref, aq_ref, ak_ref, av_ref, bq_ref, bk_ref, bv_ref, cqn_ref, cqr_ref,
                kx0_ref, kx1_ref, cv_ref, ya_ref, yb_ref, yc_ref):
    ak, av = ak_ref[...], av_ref[...]
    cqr = cqr_ref[...]
    for t in range(2):
        cols = slice(t * LANES, (t + 1) * LANES)
        qs = _stack_pair(aq_ref[:, cols])
        o = _softmax_pv((_dot_nt(qs, ak),), (av,), extra=_sink_column(sink_ref, (t, t + 2), CTX_LEN))
        ya_ref[:, cols] = _bf(_unstack_pair(o))
        qs = _stack_pair(bq_ref[:, cols])
        yb_ref[:, cols] = _bf(_unstack_pair(_softmax_pv((_dot_nt(qs, bk_ref[:, cols]),), (bv_ref[:, cols],))))
        qs = _c_pair_queries(cqn_ref[:, cols], cqr, t)
        s = _dot_nt(qs, (kx0_ref, kx1_ref)[t][...]) * C_SCALE_LOG2E
        yc_ref[:, cols] = _bf(_unstack_pair(_softmax_pv((s,), (cv_ref[:, cols],), base2=True)))


def _mixers_ctx(pc, sink_l):
    blk = lambda w, off: pl.BlockSpec((CTX_LEN, w), lambda b: (b, off // w))
    yspec = pl.BlockSpec((CTX_LEN, 256), lambda b: (b, 0))
    yshape = jax.ShapeDtypeStruct((NC, 256), jnp.bfloat16)
    return pl.pallas_call(
        _ctx_kernel,
        grid=(BATCH,),
        in_specs=[
            pl.BlockSpec(memory_space=pltpu.SMEM),
            blk(256, P_AQ), blk(LANES, P_AK), blk(LANES, P_AV),
            blk(256, P_BQ), blk(256, P_BK), blk(256, P_BV),
            blk(256, P_CQN), blk(LANES, P_CQR), blk(256, P_CKX0), blk(256, P_CKX1), blk(256, P_CV),
        ],
        out_specs=[yspec, yspec, yspec],
        out_shape=[yshape, yshape, yshape],
        compiler_params=pltpu.CompilerParams(vmem_limit_bytes=VMEM_LIMIT),
        name="mixers_ctx",
    )(sink_l, *([pc] * 11))


def _post_kernel(x_ref, mod_ref, ya_ref, yb_ref, yc_ref, yd_ref, wo_ref, g1_ref, b1_ref,
                 wgu_ref, wd_ref, g2_ref, b2_ref, o_ref):
    m = lambda j: mod_ref[:, j * D_MODEL:(j + 1) * D_MODEL]

    def mixer_out(c, _):
        rows = slice(c * POST_CHAIN, (c + 1) * POST_CHAIN)
        ycat = jnp.concatenate([ya_ref[rows, :], yb_ref[rows, :], yc_ref[rows, :], yd_ref[rows, :]], axis=1)
        y = _dot(ycat, wo_ref[...])
        x1 = _layer_norm(DN_ALPHA * x_ref[rows, :] + m(2) * y, g1_ref[...], b1_ref[...])
        return x1, _bf(x1 * (1.0 + m(4)) + m(3))

    def ffn(c, carry):
        x1, h = carry
        acc = jnp.zeros((POST_CHAIN, D_MODEL), jnp.float32)
        for j in range(FF_HIDDEN // FF_CHUNK):
            g = _dot(h, wgu_ref[:, j * FF_CHUNK:(j + 1) * FF_CHUNK])
            u = _dot(h, wgu_ref[:, FF_HIDDEN + j * FF_CHUNK:FF_HIDDEN + (j + 1) * FF_CHUNK])
            a = _bf(g * (1.0 / (1.0 + jnp.exp(-g))) * u)
            acc = acc + _dot(a, wd_ref[j * FF_CHUNK:(j + 1) * FF_CHUNK, :])
        o_ref[c * POST_CHAIN:(c + 1) * POST_CHAIN, :] = _layer_norm(
            DN_ALPHA * x1 + m(5) * acc, g2_ref[...], b2_ref[...])

    _pipelined(TM // POST_CHAIN, (mixer_out, ffn))


def _stage_post(l, x2d, mod_row, mods, ya, yb, yc, p, wo, g1, b1, wgu, wd, g2, b2):
    n = x2d.shape[0]
    full = lambda shape: pl.BlockSpec((None,) + shape, lambda i: (l,) + (0,) * len(shape))
    yspec = pl.BlockSpec((TM, 256), lambda i: (i, 0))
    return pl.pallas_call(
        _post_kernel,
        grid=(n // TM,),
        in_specs=[
            pl.BlockSpec((TM, D_MODEL), lambda i: (i, 0)),
            pl.BlockSpec((None, None, 1, 6 * D_MODEL), lambda i: (l, mod_row(i), 0, 0)),
            yspec, yspec, yspec,
            pl.BlockSpec((TM, 256), lambda i: (i, P_YD // 256)),
            full((D_MODEL, D_MODEL)), full((1, D_MODEL)), full((1, D_MODEL)),
            pl.BlockSpec((D_MODEL, 2 * FF_HIDDEN), lambda i: (0, 0)), pl.BlockSpec((FF_HIDDEN, D_MODEL), lambda i: (0, 0)),
            full((1, D_MODEL)), full((1, D_MODEL)),
        ],
        out_specs=pl.BlockSpec((TM, D_MODEL), lambda i: (i, 0)),
        out_shape=jax.ShapeDtypeStruct((n, D_MODEL), jnp.float32),
        compiler_params=pltpu.CompilerParams(vmem_limit_bytes=VMEM_LIMIT),
        name="stage_post",
    )(x2d, mods, ya, yb, yc, p, wo, g1, b1, wgu, wd, g2, b2)


def _rope_tables():
    t = jnp.arange(SEQ)
    pos = jnp.stack([t // GRID_W, t % GRID_W], 0).astype(jnp.float32)

    def table(width):
        half, quarter = width // 2, width // 4
        inv = ROPE_BASE ** (-jnp.arange(0, half, 2, dtype=jnp.float32) / half)
        lane = np.arange(LANES) % width
        axis = lane // half
        k = (lane % half) % quarter
        first = (lane % half) < quarter
        ang = pos[axis, :].T * inv[k][None, :]
        cos = jnp.cos(ang)
        sin = jnp.where(first[None, :], -jnp.sin(ang), jnp.sin(ang))
        return cos, sin

    return (*table(HEAD_DIM), *table(C_ROPE))


def kernel(x, c, ctx, c_ctx, w_mod, b_mod, w_in, a_sink, b_rpb, c_q_norm, c_kv_norm, c_w_uq, c_w_ukv,
           d_ln_g, d_ln_b, d_ws, d_bs, w_out, ln1_g, ln1_b, w_gu, w_down, ln2_g, ln2_b):
    f32 = jnp.float32
    xl = x.reshape(NX, D_MODEL)
    xc = ctx.reshape(NC, D_MODEL)
    c16 = jnp.concatenate([c, c_ctx[None, :], jnp.zeros((16 - BATCH - 1, D_MODEL), f32)], axis=0)
    mods = _modulation(c16, w_mod, b_mod).reshape(DEPTH, 16, 1, 6 * D_MODEL)

    w1, wuq, wukv, wo = _prepare_weights(w_in, c_w_uq, c_w_ukv, w_out)
    ws = _bf(d_ws)
    bs_exp = jnp.repeat(jnp.swapaxes(d_bs, 1, 2), HEAD_DIM, axis=2)
    row = lambda a: a[:, None, :]
    tabs = _rope_tables()
    tabs_ctx = (jnp.ones((TM, LANES), f32), jnp.zeros((TM, LANES), f32)) * 2
    lat_mod = lambda i: i // (SEQ // TM)
    lat_tab = lambda i: i % (SEQ // TM)
    ctx_mod = lambda i: BATCH
    ctx_tab = lambda i: 0

    for l in range(DEPTH):
        last = l == DEPTH - 1
        in_params = (mods, w1, row(c_q_norm), row(c_kv_norm), wuq, wukv, row(d_ln_g), row(d_ln_b), ws, bs_exp)
        p, wgu, wd = _stage_in(l, xl, lat_mod, tabs, lat_tab, *in_params, ffn_f32=(w_gu, w_down))
        (pc,) = _stage_in(l, xc, ctx_mod, tabs_ctx, ctx_tab, *in_params)
        post_params = (wo, row(ln1_g), row(ln1_b), wgu, wd, row(ln2_g), row(ln2_b))
        ya = _mixer_a(p, pc, a_sink[l])
        yb = _mixer_b(p, pc, _bias_tables(b_rpb[l]))
        yc = _mixer_c(p, pc)
        xl = _stage_post(l, xl, lat_mod, mods, ya, yb, yc, p, *post_params)
        if not last:
            xc = _stage_post(l, xc, ctx_mod, mods, *_mixers_ctx(pc, a_sink[l]), pc, *post_params)
    return xl.reshape(BATCH, SEQ, D_MODEL)
```

```python
import functools

import numpy as np
import jax
import jax.numpy as jnp
from jax import lax
from jax.experimental import pallas as pl
from jax.experimental.pallas import tpu as pltpu

D_MODEL = 1024
BATCH = 8
SEQ = 2048
DEPTH = 4
CTX_LEN = 256
GRID_W = 64
HEAD_DIM = 64
A_WINDOW = 128
B_WIN_H = 8
B_WIN_W = 16
C_Q_RANK = 256
C_KV_RANK = 128
C_NOPE = 64
C_ROPE = 32
FF_HIDDEN = 2816
ROPE_BASE = 10000.0
LN_EPS = 1e-6
NEG_INF = -1e30
DN_ALPHA = (2 * DEPTH) ** 0.25

N_ROWS = SEQ // GRID_W
NX = BATCH * SEQ
NC = BATCH * CTX_LEN

LANES = 128
VMEM_LIMIT = 56 * 1024 * 1024

TM = 1024
CHAIN = 512
POST_CHAIN = 512
A_BLK = 128
A_BAND = 3 * A_BLK
B_QROWS = 2
B_QBLK = B_QROWS * GRID_W
B_KROWS = B_WIN_H + B_QROWS
B_KBLK = B_KROWS * GRID_W
B_STEPS = N_ROWS // B_QROWS
B_CLASSES = 5
A_SUB = 16
B_SUB = 16
C_QBLK = 1024
C_SUBBLK = 128
FF_CHUNK = 256

P_AQ, P_BK, P_BV, P_BQ, P_CKX0, P_CKX1, P_CV, P_CQN, P_YD = (256 * i for i in range(9))
P_AK, P_AV, P_CQR = 2304, 2432, 2560
P_W = 2688
W1_AQ, W1_BK, W1_BV, W1_BQ, W1_CCQ, W1_DU, W1_DV, W1_AKV, W1_CKV = (256 * i for i in range(9))
W1_W = 2304


def _bf(x):
    return x.astype(jnp.bfloat16)


def _dot(a, b):
    return jnp.dot(a, b, preferred_element_type=jnp.float32)


def _dot_nt(a, b):
    return lax.dot_general(a, b, (((1,), (1,)), ((), ())), preferred_element_type=jnp.float32)


def _lane_lt(n, limit):
    return lax.broadcasted_iota(jnp.int32, (1, n), 1) < limit


def _half_mask(c):
    lane = lax.broadcasted_iota(jnp.int32, (1, LANES), 1)
    return (lane >= c * HEAD_DIM) & (lane < (c + 1) * HEAD_DIM)


def _gelu(x):
    return x * (0.5 * (1.0 + jnp.tanh(np.sqrt(2.0 / np.pi).astype(np.float32) * (x + 0.044715 * (x * x * x)))))


def _layer_norm(x, g, b):
    xc = x - jnp.mean(x, axis=-1, keepdims=True)
    var = jnp.mean(xc * xc, axis=-1, keepdims=True)
    return xc * lax.rsqrt(var + LN_EPS) * g + b


def _rms_norm(x, g):
    return x * lax.rsqrt(jnp.mean(x * x, axis=-1, keepdims=True) + LN_EPS) * g


def _rope(x, cos, sin_signed, half):
    lane = lax.broadcasted_iota(jnp.int32, (1, LANES), 1)
    first = (lane % (2 * half)) < half
    partner = jnp.where(first, pltpu.roll(x, LANES - half, 1), pltpu.roll(x, half, 1))
    return x * cos + partner * sin_signed


def _softmax_num(scores, extra=None, base2=False):
    ex = jnp.exp2 if base2 else jnp.exp
    s = scores[0] if len(scores) == 1 else jnp.concatenate(scores, axis=1)
    m = jnp.max(s, axis=-1, keepdims=True)
    if extra is None:
        return _bf(ex(s - m)), None
    m = jnp.maximum(m, extra)
    return _bf(ex(s - m)), ex(extra - m)


def _pv_normalised(p, extra_num, values):
    v = values[0] if len(values) == 1 else jnp.concatenate(values, axis=0)
    oe = _dot(p, jnp.concatenate([v, jnp.ones_like(v)], axis=1))
    o, l = oe[:, :LANES], oe[:, LANES:]
    return o / (l if extra_num is None else l + extra_num)


def _softmax_pv(scores, values, extra=None, base2=False):
    return _pv_normalised(*_softmax_num(scores, extra, base2), values)


def _pipelined(n, stages):
    stages = [s if callable(s) else (lambda i, c, fs=s: functools.reduce(lambda acc, f: f(i, acc), fs, c))
              for s in stages]
    depth = len(stages)
    carry = [None] * n
    for tick in range(n + depth - 1):
        for k in range(depth):
            i = tick - k
            if 0 <= i < n:
                carry[i] = stages[k](i, carry[i])


MOD_TN = 1536


def _mod_kernel(c_ref, w_ref, b_ref, o_ref):
    c = c_ref[...]
    s = c * (1.0 / (1.0 + jnp.exp(-c)))
    o_ref[...] = _dot(_bf(s), _bf(w_ref[...])) + b_ref[...]


def _modulation(c16, w_mod, b_mod):
    n = 6 * D_MODEL
    return pl.pallas_call(
        _mod_kernel,
        grid=(DEPTH, n // MOD_TN),
        in_specs=[
            pl.BlockSpec((16, D_MODEL), lambda l, j: (0, 0)),
            pl.BlockSpec((None, D_MODEL, MOD_TN), lambda l, j: (l, 0, j)),
            pl.BlockSpec((None, 1, MOD_TN), lambda l, j: (l, 0, j)),
        ],
        out_specs=pl.BlockSpec((None, 16, MOD_TN), lambda l, j: (l, 0, j)),
        out_shape=jax.ShapeDtypeStruct((DEPTH, 16, n), jnp.float32),
        compiler_params=pltpu.CompilerParams(vmem_limit_bytes=VMEM_LIMIT),
        name="modulation",
    )(c16, w_mod, b_mod.reshape(DEPTH, 1, n))


def _take(w, axis, pieces):
    return jnp.concatenate([lax.slice_in_dim(w, off, off + n, axis=axis) for off, n in pieces], axis=axis)


def _w1_pieces():
    ak, av, bk, bv, cckv, ckr, aq, bq, ccq, du, dv = np.cumsum((0, 128, 128, 256, 256, 128, 32, 256, 256, 256, 256))[:11]
    return ([(aq + 64 * h, 64) for h in (0, 2, 1, 3)]
            + [(bk, 256), (bv, 256), (bq, 256), (ccq, 256), (du, 256), (dv, 256), (ak, 128), (av, 128), (cckv, 128)]
            + [(ckr, 32)] * 4)


_UQ_PIECES = [(96 * h, 64) for h in range(4)] + [(96 * h + 64, 32) for h in range(4)]
_UKV_PIECES = [(128 * h, 64) for h in range(4)] + [(128 * h + 64, 64) for h in range(4)]
_WO_PIECES = [(64 * h, 64) for h in (0, 2, 1, 3)] + [(4 * HEAD_DIM, D_MODEL - 4 * HEAD_DIM)]


def _prep_kernel(wint_ref, wuq_ref, wukv_ref, wout_ref, w1t_ref, uq_ref, ukv_ref, wo_ref):
    w1t_ref[...] = _bf(_take(wint_ref[...], 0, _w1_pieces()))
    wo_ref[...] = _bf(_take(wout_ref[...], 0, _WO_PIECES))
    uq_ref[...] = _bf(_take(wuq_ref[...], 1, _UQ_PIECES))
    ukv_ref[...] = _bf(_take(wukv_ref[...], 1, _UKV_PIECES))


def _prepare_weights(w_in_t, c_w_uq, c_w_ukv, w_out):
    whole = lambda a: pl.BlockSpec((None,) + a.shape[1:], lambda l: (l, 0, 0))
    w1t_shape = jax.ShapeDtypeStruct((DEPTH, W1_W, D_MODEL), jnp.bfloat16)
    outs = [w1t_shape] + [jax.ShapeDtypeStruct(a.shape, jnp.bfloat16) for a in (c_w_uq, c_w_ukv, w_out)]
    return pl.pallas_call(
        _prep_kernel,
        grid=(DEPTH,),
        in_specs=[whole(a) for a in (w_in_t, c_w_uq, c_w_ukv, w_out)],
        out_specs=[whole(a) for a in outs],
        out_shape=outs,
        compiler_params=pltpu.CompilerParams(vmem_limit_bytes=VMEM_LIMIT),
        name="prepare_weights",
    )(w_in_t, c_w_uq, c_w_ukv, w_out)


def _b_step_geometry(step):
    r0 = step * B_QROWS
    w0 = min(max(r0 - B_WIN_H // 2, 0), N_ROWS - B_KROWS)
    return r0, w0


_B_CLASS_STEPS = (0, 1, 2, B_STEPS - 2, B_STEPS - 1)


def _bias_kernel(rpb_ref, o_ref):
    n_dr, n_dc = 2 * B_WIN_H - 1, 2 * B_WIN_W - 1
    cq = lax.broadcasted_iota(jnp.int32, (GRID_W, LANES), 0)
    ck = lax.broadcasted_iota(jnp.int32, (GRID_W, LANES), 1) % GRID_W
    c_start = jnp.clip(cq - B_WIN_W // 2, 0, GRID_W - B_WIN_W)
    col_ok = (ck >= c_start) & (ck < c_start + B_WIN_W)
    dc = jnp.clip(ck - cq + (B_WIN_W - 1), 0, n_dc - 1)
    neg = jnp.full((GRID_W, LANES), NEG_INF, jnp.float32)
    lane_lo = lax.broadcasted_iota(jnp.int32, (GRID_W, LANES), 1) < GRID_W
    for h in range(4):
        tz = []
        for dr in range(n_dr):
            t = jnp.zeros((GRID_W, LANES), jnp.float32)
            for d in range(n_dc):
                t = jnp.where(dc == d, rpb_ref[(h * n_dr + dr) * n_dc + d], t)
            tz.append(jnp.where(col_ok, t, neg))
        for ci, step in enumerate(_B_CLASS_STEPS):
            r0, w0 = _b_step_geometry(step)
            for qr in range(B_QROWS):
                r = r0 + qr
                r_start = min(max(r - B_WIN_H // 2, 0), N_ROWS - B_WIN_H)
                for kp in range(B_KROWS // 2):
                    halves = []
                    for kr in (2 * kp, 2 * kp + 1):
                        rk = w0 + kr
                        ok = r_start <= rk < r_start + B_WIN_H
                        halves.append(tz[rk - r + (B_WIN_H - 1)] if ok else neg)
                    o_ref[ci, h, qr * GRID_W:(qr + 1) * GRID_W, kp * LANES:(kp + 1) * LANES] = (
                        jnp.where(lane_lo, halves[0], halves[1]))


def _bias_tables(rpb_l):
    return pl.pallas_call(
        _bias_kernel,
        in_specs=[pl.BlockSpec(memory_space=pltpu.SMEM)],
        out_specs=pl.BlockSpec(memory_space=pltpu.VMEM),
        out_shape=jax.ShapeDtypeStruct((B_CLASSES, 4, B_QBLK, B_KBLK), jnp.float32),
        compiler_params=pltpu.CompilerParams(vmem_limit_bytes=VMEM_LIMIT),
        name="b_bias_tables",
    )(rpb_l.reshape(-1))


def _in_kernel(x_ref, mod_ref, w1_ref, cosa_ref, sina_ref, cosc_ref, sinc_ref, qn_ref, kvn_ref,
               wuq_ref, wukv_ref, dg_ref, db_ref, ws_ref, bs_ref, *rest, cast_ffn):
    if cast_ffn:
        wgu_ref, wd_ref, p_ref, wgu_out_ref, wd_out_ref = rest
        wgu_out_ref[...] = _bf(wgu_ref[...])
        wd_out_ref[...] = _bf(wd_ref[...])
    else:
        (p_ref,) = rest
    shift = mod_ref[:, 0:D_MODEL]
    scale = mod_ref[:, D_MODEL:2 * D_MODEL]
    qscale = HEAD_DIM ** -0.5
    lo = _lane_lt(LANES, HEAD_DIM)

    def projections(c, _):
        rows = slice(c * CHAIN, (c + 1) * CHAIN)
        h = _bf(x_ref[rows, :] * (1.0 + scale) + shift)
        proj = lambda off: _dot_nt(h, w1_ref[off:off + 256, :])
        cosa, sina = cosa_ref[rows, :], sina_ref[rows, :]
        aq = proj(W1_AQ)
        for t in range(2):
            sl = slice(t * LANES, (t + 1) * LANES)
            p_ref[rows, P_AQ + t * LANES:P_AQ + (t + 1) * LANES] = _bf(_rope(aq[:, sl], cosa, sina, 16) * qscale)
        p_ref[rows, P_BK:P_BK + 256] = _bf(proj(W1_BK))
        p_ref[rows, P_BV:P_BV + 256] = _bf(proj(W1_BV))
        p_ref[rows, P_BQ:P_BQ + 256] = _bf(proj(W1_BQ) * qscale)
        akv = proj(W1_AKV)
        p_ref[rows, P_AK:P_AK + LANES] = _bf(_rope(akv[:, :LANES], cosa, sina, 16))
        p_ref[rows, P_AV:P_AV + LANES] = _bf(akv[:, LANES:])
        return proj(W1_CKV), proj(W1_CCQ), proj(W1_DU), proj(W1_DV)

    def mixer_prep(c, carry):
        rows = slice(c * CHAIN, (c + 1) * CHAIN)
        ckv, ccq, du, dv = carry
        cosc, sinc = cosc_ref[rows, :], sinc_ref[rows, :]
        kr4 = _bf(_rope(ckv[:, LANES:], cosc, sinc, 8))
        kv = _dot(_bf(_rms_norm(ckv[:, :LANES], kvn_ref[...])), wukv_ref[...])
        p_ref[rows, P_CKX0:P_CKX0 + LANES] = _bf(kv[:, 0:LANES])
        p_ref[rows, P_CKX0 + LANES:P_CKX0 + 256] = kr4
        p_ref[rows, P_CKX1:P_CKX1 + LANES] = _bf(kv[:, LANES:2 * LANES])
        p_ref[rows, P_CKX1 + LANES:P_CKX1 + 256] = kr4
        p_ref[rows, P_CV:P_CV + 256] = _bf(kv[:, 256:512])
        qq = _dot(_bf(_rms_norm(ccq, qn_ref[...])), wuq_ref[...])
        p_ref[rows, P_CQN:P_CQN + 256] = _bf(qq[:, 0:256])
        p_ref[rows, P_CQR:P_CQR + LANES] = _bf(_rope(qq[:, 256:384], cosc, sinc, 8))
        u = _gelu(du)
        v = _bf(_layer_norm(_gelu(dv), dg_ref[...], db_ref[...]))
        for ch in range(CHAIN // 128):
            sub = slice(ch * 128, (ch + 1) * 128)
            out_rows = slice(c * CHAIN + ch * 128, c * CHAIN + (ch + 1) * 128)
            for pr in range(2):
                cols = slice(pr * LANES, (pr + 1) * LANES)
                vt = v[sub, cols]
                mixed = jnp.where(lo, _dot(ws_ref[2 * pr], vt), _dot(ws_ref[2 * pr + 1], vt))
                p_ref[out_rows, P_YD + pr * LANES:P_YD + (pr + 1) * LANES] = _bf(
                    u[sub, cols] * (mixed + bs_ref[:, cols]))

    _pipelined(TM // CHAIN, (projections, mixer_prep))


def _stage_in(l, x2d, mod_row, tabs, tab_block, mods, w1, qn, kvn, wuq, wukv, dg, db, ws, bs_exp, ffn_f32=None):
    n = x2d.shape[0]
    steps = n // TM
    full = lambda shape: pl.BlockSpec((None,) + shape, lambda i: (l,) + (0,) * len(shape))
    tab = pl.BlockSpec((TM, LANES), lambda i: (tab_block(i), 0))
    in_specs = [
        pl.BlockSpec((TM, D_MODEL), lambda i: (i, 0)),
        pl.BlockSpec((None, None, 1, 6 * D_MODEL), lambda i: (l, mod_row(i), 0, 0)),
        full((W1_W, D_MODEL)),
        tab, tab, tab, tab,
        full((1, C_Q_RANK)), full((1, C_KV_RANK)),
        full((C_Q_RANK, 384)), full((C_KV_RANK, 512)),
        full((1, 256)), full((1, 256)),
        full((4, 128, 128)), full((128, 256)),
    ]
    out_specs = [pl.BlockSpec((TM, P_W), lambda i: (i, 0))]
    out_shape = [jax.ShapeDtypeStruct((n, P_W), jnp.bfloat16)]
    operands = [x2d, mods, w1, *tabs, qn, kvn, wuq, wukv, dg, db, ws, bs_exp]
    if ffn_f32 is not None:
        for w in ffn_f32:
            rows, cols = w.shape[1] // steps, w.shape[2]
            in_specs.append(pl.BlockSpec((None, rows, cols), lambda i: (l, i, 0)))
            out_specs.append(pl.BlockSpec((rows, cols), lambda i: (i, 0)))
            out_shape.append(jax.ShapeDtypeStruct(w.shape[1:], jnp.bfloat16))
            operands.append(w)
    return pl.pallas_call(
        functools.partial(_in_kernel, cast_ffn=ffn_f32 is not None),
        grid=(steps,),
        in_specs=in_specs,
        out_specs=out_specs,
        out_shape=out_shape,
        compiler_params=pltpu.CompilerParams(vmem_limit_bytes=VMEM_LIMIT),
        name="stage_in",
    )(*operands)


def _mask_half(q_tile, c):
    return jnp.where(_half_mask(c), q_tile, jnp.zeros_like(q_tile))


def _stack_pair(q_tile):
    return jnp.concatenate([_mask_half(q_tile, 0), _mask_half(q_tile, 1)], axis=0)


def _unstack_pair(o):
    n = o.shape[0] // 2
    return jnp.where(_lane_lt(LANES, HEAD_DIM), o[:n], o[n:])


def _sink_column(sink_ref, heads, n):
    return jnp.concatenate([jnp.full((n, 1), sink_ref[h], jnp.float32) for h in heads], axis=0)


def _a_kernel(sink_ref, q_ref, k_ref, v_ref, kc_ref, vc_ref, o_ref):
    g = pl.program_id(1)
    kc, vc = kc_ref[...], vc_ref[...]
    sink = _sink_column(sink_ref, (0, 2, 1, 3), A_BLK)

    def band_start(j):
        return pl.multiple_of(jnp.clip((g * A_SUB + j - 1) * A_BLK, 0, SEQ - A_BAND), A_BLK)

    def scores(j, _):
        rows = slice(j * A_BLK, (j + 1) * A_BLK)
        start = band_start(j)
        qpos = (g * A_SUB + j) * A_BLK + lax.broadcasted_iota(jnp.int32, (A_BLK, A_BAND), 0)
        kpos = start + lax.broadcasted_iota(jnp.int32, (A_BLK, A_BAND), 1)
        valid = jnp.abs(qpos - kpos) <= A_WINDOW
        qs = jnp.concatenate([_stack_pair(q_ref[rows, 0:LANES]), _stack_pair(q_ref[rows, LANES:256])], axis=0)
        s = _dot_nt(qs, k_ref[pl.ds(start, A_BAND), :]).reshape(4, A_BLK, A_BAND)
        s = jnp.where(valid[None], s, NEG_INF).reshape(4 * A_BLK, A_BAND)
        return s, _dot_nt(qs, kc)

    def softmax(j, s):
        return _softmax_num(s, extra=sink)

    def finish(j, num):
        rows = slice(j * A_BLK, (j + 1) * A_BLK)
        o = _pv_normalised(*num, (v_ref[pl.ds(band_start(j), A_BAND), :], vc))
        o_ref[rows, 0:LANES] = _bf(_unstack_pair(o[:2 * A_BLK]))
        o_ref[rows, LANES:256] = _bf(_unstack_pair(o[2 * A_BLK:]))

    _pipelined(A_SUB, (scores, (softmax, finish)))


def _mixer_a(p, pc, sink_l):
    ng = SEQ // (A_BLK * A_SUB)
    return pl.pallas_call(
        _a_kernel,
        grid=(BATCH, ng),
        in_specs=[
            pl.BlockSpec(memory_space=pltpu.SMEM),
            pl.BlockSpec((A_BLK * A_SUB, 256), lambda b, n: (b * ng + n, P_AQ // 256)),
            pl.BlockSpec((SEQ, LANES), lambda b, n: (b, P_AK // LANES)),
            pl.BlockSpec((SEQ, LANES), lambda b, n: (b, P_AV // LANES)),
            pl.BlockSpec((CTX_LEN, LANES), lambda b, n: (b, P_AK // LANES)),
            pl.BlockSpec((CTX_LEN, LANES), lambda b, n: (b, P_AV // LANES)),
        ],
        out_specs=pl.BlockSpec((A_BLK * A_SUB, 256), lambda b, n: (b * ng + n, 0)),
        out_shape=jax.ShapeDtypeStruct((NX, 256), jnp.bfloat16),
        compiler_params=pltpu.CompilerParams(vmem_limit_bytes=VMEM_LIMIT),
        name="mixer_a",
    )(sink_l, p, p, p, pc, pc)


def _b_class(step):
    return jnp.where(step < 2, step, jnp.where(step >= B_STEPS - 2, step - (B_STEPS - B_CLASSES), 2))


def _b_kernel(q_ref, k_ref, v_ref, kc_ref, vc_ref, bias_ref, o_ref):
    g = pl.program_id(1)

    def where(i):
        j, t = i // 2, i % 2
        step = g * B_SUB + j
        w0 = jnp.clip(step * B_QROWS - B_WIN_H // 2, 0, N_ROWS - B_KROWS)
        return (step, slice(j * B_QBLK, (j + 1) * B_QBLK), slice(t * LANES, (t + 1) * LANES), t,
                pl.multiple_of(w0 * GRID_W, GRID_W))

    def scores(i, _):
        step, rows, cols, t, start = where(i)
        qs = _stack_pair(q_ref[rows, cols])
        bias = bias_ref[_b_class(step), 2 * t:2 * t + 2].reshape(2 * B_QBLK, B_KBLK)
        return _dot_nt(qs, k_ref[pl.ds(start, B_KBLK), cols]) + bias, _dot_nt(qs, kc_ref[:, cols])

    def softmax(i, s):
        return _softmax_num(s)

    def finish(i, num):
        _, rows, cols, _, start = where(i)
        o = _pv_normalised(*num, (v_ref[pl.ds(start, B_KBLK), cols], vc_ref[:, cols]))
        o_ref[rows, cols] = _bf(_unstack_pair(o))

    _pipelined(2 * B_SUB, (scores, (softmax, finish)))


def _mixer_b(p, pc, bias_tab):
    ng = B_STEPS // B_SUB
    return pl.pallas_call(
        _b_kernel,
        grid=(BATCH, ng),
        in_specs=[
            pl.BlockSpec((B_QBLK * B_SUB, 256), lambda b, s: (b * ng + s, P_BQ // 256)),
            pl.BlockSpec((SEQ, 256), lambda b, s: (b, P_BK // 256)),
            pl.BlockSpec((SEQ, 256), lambda b, s: (b, P_BV // 256)),
            pl.BlockSpec((CTX_LEN, 256), lambda b, s: (b, P_BK // 256)),
            pl.BlockSpec((CTX_LEN, 256), lambda b, s: (b, P_BV // 256)),
            pl.BlockSpec((B_CLASSES, 4, B_QBLK, B_KBLK), lambda b, s: (0, 0, 0, 0)),
        ],
        out_specs=pl.BlockSpec((B_QBLK * B_SUB, 256), lambda b, s: (b * ng + s, 0)),
        out_shape=jax.ShapeDtypeStruct((NX, 256), jnp.bfloat16),
        compiler_params=pltpu.CompilerParams(vmem_limit_bytes=VMEM_LIMIT),
        name="mixer_b",
    )(p, p, p, pc, pc, bias_tab)


C_SCALE_LOG2E = float((C_NOPE + C_ROPE) ** -0.5 * np.log2(np.e))


def _c_query(qn_tile, qr_tile, h):
    lane = lax.broadcasted_iota(jnp.int32, (1, LANES), 1)
    quarter = (lane >= h * C_ROPE) & (lane < (h + 1) * C_ROPE)
    return jnp.concatenate([_mask_half(qn_tile, h % 2),
                            jnp.where(quarter, qr_tile, jnp.zeros_like(qr_tile))], axis=1)


def _c_pair_queries(qn_tile, qr_tile, t):
    return jnp.concatenate([_c_query(qn_tile, qr_tile, 2 * t), _c_query(qn_tile, qr_tile, 2 * t + 1)], axis=0)


def _c_kernel(qn_ref, qr_ref, kx0_ref, kx1_ref, v_ref, kx0c_ref, kx1c_ref, vc_ref, o_ref):
    kx_refs = ((kx0_ref, kx0c_ref), (kx1_ref, kx1c_ref))

    def where(i):
        j, t = i // 2, i % 2
        return slice(j * C_SUBBLK, (j + 1) * C_SUBBLK), slice(t * LANES, (t + 1) * LANES), t

    def scores(i, _):
        rows, cols, t = where(i)
        qs = _c_pair_queries(qn_ref[rows, cols], qr_ref[rows, :], t)
        return tuple(_dot_nt(qs, r[...]) * C_SCALE_LOG2E for r in kx_refs[t])

    def softmax(i, s):
        return _softmax_num(s, base2=True)

    def finish(i, num):
        rows, cols, _ = where(i)
        o_ref[rows, cols] = _bf(_unstack_pair(_pv_normalised(*num, (v_ref[:, cols], vc_ref[:, cols]))))

    _pipelined(2 * (C_QBLK // C_SUBBLK), (scores, (softmax, finish)))


def _mixer_c(p, pc):
    nq = SEQ // C_QBLK
    return pl.pallas_call(
        _c_kernel,
        grid=(BATCH, nq),
        in_specs=[
            pl.BlockSpec((C_QBLK, 256), lambda b, n: (b * nq + n, P_CQN // 256)),
            pl.BlockSpec((C_QBLK, LANES), lambda b, n: (b * nq + n, P_CQR // LANES)),
            pl.BlockSpec((SEQ, 256), lambda b, n: (b, P_CKX0 // 256)),
            pl.BlockSpec((SEQ, 256), lambda b, n: (b, P_CKX1 // 256)),
            pl.BlockSpec((SEQ, 256), lambda b, n: (b, P_CV // 256)),
            pl.BlockSpec((CTX_LEN, 256), lambda b, n: (b, P_CKX0 // 256)),
            pl.BlockSpec((CTX_LEN, 256), lambda b, n: (b, P_CKX1 // 256)),
            pl.BlockSpec((CTX_LEN, 256), lambda b, n: (b, P_CV // 256)),
        ],
        out_specs=pl.BlockSpec((C_QBLK, 256), lambda b, n: (b * nq + n, 0)),
        out_shape=jax.ShapeDtypeStruct((NX, 256), jnp.bfloat16),
        compiler_params=pltpu.CompilerParams(vmem_limit_bytes=VMEM_LIMIT),
        name="mixer_c",
    )(p, p, p, p, p, pc, pc, pc)


def _ctx_kernel(sink_ref, aq_ref, ak_ref, av_ref, bq_ref, bk_ref, bv_ref, cqn_ref, cqr_ref,
                kx0_ref, kx1_ref, cv_ref, ya_ref, yb_ref, yc_ref):
    ak, av = ak_ref[...], av_ref[...]
    cqr = cqr_ref[...]
    for t in range(2):
        cols = slice(t * LANES, (t + 1) * LANES)
        qs = _stack_pair(aq_ref[:, cols])
        o = _softmax_pv((_dot_nt(qs, ak),), (av,), extra=_sink_column(sink_ref, (t, t + 2), CTX_LEN))
        ya_ref[:, cols] = _bf(_unstack_pair(o))
        qs = _stack_pair(bq_ref[:, cols])
        yb_ref[:, cols] = _bf(_unstack_pair(_softmax_pv((_dot_nt(qs, bk_ref[:, cols]),), (bv_ref[:, cols],))))
        qs = _c_pair_queries(cqn_ref[:, cols], cqr, t)
        s = _dot_nt(qs, (kx0_ref, kx1_ref)[t][...]) * C_SCALE_LOG2E
        yc_ref[:, cols] = _bf(_unstack_pair(_softmax_pv((s,), (cv_ref[:, cols],), base2=True)))


def _mixers_ctx(pc, sink_l):
    blk = lambda w, off: pl.BlockSpec((CTX_LEN, w), lambda b: (b, off // w))
    yspec = pl.BlockSpec((CTX_LEN, 256), lambda b: (b, 0))
    yshape = jax.ShapeDtypeStruct((NC, 256), jnp.bfloat16)
    return pl.pallas_call(
        _ctx_kernel,
        grid=(BATCH,),
        in_specs=[
            pl.BlockSpec(memory_space=pltpu.SMEM),
            blk(256, P_AQ), blk(LANES, P_AK), blk(LANES, P_AV),
            blk(256, P_BQ), blk(256, P_BK), blk(256, P_BV),
            blk(256, P_CQN), blk(LANES, P_CQR), blk(256, P_CKX0), blk(256, P_CKX1), blk(256, P_CV),
        ],
        out_specs=[yspec, yspec, yspec],
        out_shape=[yshape, yshape, yshape],
        compiler_params=pltpu.CompilerParams(vmem_limit_bytes=VMEM_LIMIT),
        name="mixers_ctx",
    )(sink_l, *([pc] * 11))


def _post_kernel(x_ref, mod_ref, ya_ref, yb_ref, yc_ref, yd_ref, wo_ref, g1_ref, b1_ref,
                 wgu_ref, wd_ref, g2_ref, b2_ref, o_ref):
    m = lambda j: mod_ref[:, j * D_MODEL:(j + 1) * D_MODEL]

    def mixer_out(c, _):
        rows = slice(c * POST_CHAIN, (c + 1) * POST_CHAIN)
        ycat = jnp.concatenate([ya_ref[rows, :], yb_ref[rows, :], yc_ref[rows, :], yd_ref[rows, :]], axis=1)
        y = _dot(ycat, wo_ref[...])
        x1 = _layer_norm(DN_ALPHA * x_ref[rows, :] + m(2) * y, g1_ref[...], b1_ref[...])
        return x1, _bf(x1 * (1.0 + m(4)) + m(3))

    def ffn(c, carry):
        x1, h = carry
        acc = jnp.zeros((POST_CHAIN, D_MODEL), jnp.float32)
        for j in range(FF_HIDDEN // FF_CHUNK):
            g = _dot(h, wgu_ref[:, j * FF_CHUNK:(j + 1) * FF_CHUNK])
            u = _dot(h, wgu_ref[:, FF_HIDDEN + j * FF_CHUNK:FF_HIDDEN + (j + 1) * FF_CHUNK])
            a = _bf(g * (1.0 / (1.0 + jnp.exp(-g))) * u)
            acc = acc + _dot(a, wd_ref[j * FF_CHUNK:(j + 1) * FF_CHUNK, :])
        o_ref[c * POST_CHAIN:(c + 1) * POST_CHAIN, :] = _layer_norm(
            DN_ALPHA * x1 + m(5) * acc, g2_ref[...], b2_ref[...])

    _pipelined(TM // POST_CHAIN, (mixer_out, ffn))


def _stage_post(l, x2d, mod_row, mods, ya, yb, yc, p, wo, g1, b1, wgu, wd, g2, b2):
    n = x2d.shape[0]
    full = lambda shape: pl.BlockSpec((None,) + shape, lambda i: (l,) + (0,) * len(shape))
    yspec = pl.BlockSpec((TM, 256), lambda i: (i, 0))
    return pl.pallas_call(
        _post_kernel,
        grid=(n // TM,),
        in_specs=[
            pl.BlockSpec((TM, D_MODEL), lambda i: (i, 0)),
            pl.BlockSpec((None, None, 1, 6 * D_MODEL), lambda i: (l, mod_row(i), 0, 0)),
            yspec, yspec, yspec,
            pl.BlockSpec((TM, 256), lambda i: (i, P_YD // 256)),
            full((D_MODEL, D_MODEL)), full((1, D_MODEL)), full((1, D_MODEL)),
            pl.BlockSpec((D_MODEL, 2 * FF_HIDDEN), lambda i: (0, 0)), pl.BlockSpec((FF_HIDDEN, D_MODEL), lambda i: (0, 0)),
            full((1, D_MODEL)), full((1, D_MODEL)),
        ],
        out_specs=pl.BlockSpec((TM, D_MODEL), lambda i: (i, 0)),
        out_shape=jax.ShapeDtypeStruct((n, D_MODEL), jnp.float32),
        compiler_params=pltpu.CompilerParams(vmem_limit_bytes=VMEM_LIMIT),
        name="stage_post",
    )(x2d, mods, ya, yb, yc, p, wo, g1, b1, wgu, wd, g2, b2)


def _rope_tables():
    t = lax.broadcasted_iota(jnp.int32, (SEQ, LANES), 0)
    grid_row, grid_col = (t // GRID_W).astype(jnp.float32), (t % GRID_W).astype(jnp.float32)

    def table(width):
        half, quarter = width // 2, width // 4
        lane = np.arange(LANES) % width
        k = (lane % half) % quarter
        inv = (np.float32(ROPE_BASE) ** (-(2 * k).astype(np.float32) / np.float32(half)))[None, :]
        on_cols = (lane // half == 1)[None, :]
        first = ((lane % half) < quarter)[None, :]
        ang = jnp.where(on_cols, grid_col, grid_row) * inv
        sin = jnp.sin(ang)
        return jnp.cos(ang), jnp.where(first, -sin, sin)

    return (*table(HEAD_DIM), *table(C_ROPE))


def kernel(x, c, ctx, c_ctx, w_mod, b_mod, w_in, a_sink, b_rpb, c_q_norm, c_kv_norm, c_w_uq, c_w_ukv,
           d_ln_g, d_ln_b, d_ws, d_bs, w_out, ln1_g, ln1_b, w_gu, w_down, ln2_g, ln2_b):
    f32 = jnp.float32
    xl = x.reshape(NX, D_MODEL)
    xc = ctx.reshape(NC, D_MODEL)
    c16 = jnp.concatenate([c, c_ctx[None, :], jnp.zeros((16 - BATCH - 1, D_MODEL), f32)], axis=0)
    mods = _modulation(c16, w_mod, b_mod).reshape(DEPTH, 16, 1, 6 * D_MODEL)

    w1, wuq, wukv, wo = _prepare_weights(jnp.swapaxes(w_in, 1, 2), c_w_uq, c_w_ukv, w_out)
    ws = _bf(d_ws)
    bs_exp = jnp.repeat(jnp.swapaxes(d_bs, 1, 2), HEAD_DIM, axis=2)
    row = lambda a: a[:, None, :]
    tabs = _rope_tables()
    tabs_ctx = (jnp.ones((TM, LANES), f32), jnp.zeros((TM, LANES), f32)) * 2
    lat_mod = lambda i: i // (SEQ // TM)
    lat_tab = lambda i: i % (SEQ // TM)
    ctx_mod = lambda i: BATCH
    ctx_tab = lambda i: 0

    for l in range(DEPTH):
        last = l == DEPTH - 1
        in_params = (mods, w1, row(c_q_norm), row(c_kv_norm), wuq, wukv, row(d_ln_g), row(d_ln_b), ws, bs_exp)
        p, wgu, wd = _stage_in(l, xl, lat_mod, tabs, lat_tab, *in_params, ffn_f32=(w_gu, w_down))
        (pc,) = _stage_in(l, xc, ctx_mod, tabs_ctx, ctx_tab, *in_params)
        post_params = (wo, row(ln1_g), row(ln1_b), wgu, wd, row(ln2_g), row(ln2_b))
        ya = _mixer_a(p, pc, a_sink[l])
        yb = _mixer_b(p, pc, _bias_tables(b_rpb[l]))
        yc = _mixer_c(p, pc)
        xl = _stage_post(l, xl, lat_mod, mods, ya, yb, yc, p, *post_params)
        if not last:
            xc = _stage_post(l, xc, ctx_mod, mods, *_mixers_ctx(pc, a_sink[l]), pc, *post_params)
    return xl.reshape(BATCH, SEQ, D_MODEL)
```

```python
import functools

import numpy as np
import jax
import jax.numpy as jnp
from jax import lax
from jax.experimental import pallas as pl
from jax.experimental.pallas import tpu as pltpu

D_MODEL = 1024
BATCH = 8
SEQ = 2048
DEPTH = 4
CTX_LEN = 256
GRID_W = 64
HEAD_DIM = 64
A_WINDOW = 128
B_WIN_H = 8
B_WIN_W = 16
C_Q_RANK = 256
C_KV_RANK = 128
C_NOPE = 64
C_ROPE = 32
FF_HIDDEN = 2816
ROPE_BASE = 10000.0
LN_EPS = 1e-6
NEG_INF = -1e30
DN_ALPHA = (2 * DEPTH) ** 0.25

N_ROWS = SEQ // GRID_W
NX = BATCH * SEQ
NC = BATCH * CTX_LEN

LANES = 128
VMEM_LIMIT = 56 * 1024 * 1024

TM = 1024
X_TILES = NX // TM
ALL_TILES = (NX + NC) // TM
CHAIN = 512
POST_CHAIN = 512
A_BLK = 128
A_BAND = 3 * A_BLK
B_QROWS = 2
B_QBLK = B_QROWS * GRID_W
B_KROWS = B_WIN_H + B_QROWS
B_KBLK = B_KROWS * GRID_W
B_STEPS = N_ROWS // B_QROWS
B_CLASSES = 5
A_SUB = 16
B_SUB = 16
C_QBLK = 1024
C_SUBBLK = 128
FF_CHUNK = 256

P_AQ, P_BK, P_BV, P_BQ, P_CKX0, P_CKX1, P_CV, P_CQN, P_YD = (256 * i for i in range(9))
P_AK, P_AV, P_CQR = 2304, 2432, 2560
P_W = 2688
W1_AQ, W1_BK, W1_BV, W1_BQ, W1_CCQ, W1_DU, W1_DV, W1_AKV, W1_CKV = (256 * i for i in range(9))
W1_W = 2304


def _bf(x):
    return x.astype(jnp.bfloat16)


def _dot(a, b):
    return jnp.dot(a, b, preferred_element_type=jnp.float32)


def _dot_nt(a, b):
    return lax.dot_general(a, b, (((1,), (1,)), ((), ())), preferred_element_type=jnp.float32)


def _lane_lt(n, limit):
    return lax.broadcasted_iota(jnp.int32, (1, n), 1) < limit


def _half_mask(c):
    lane = lax.broadcasted_iota(jnp.int32, (1, LANES), 1)
    return (lane >= c * HEAD_DIM) & (lane < (c + 1) * HEAD_DIM)


def _gelu(x):
    return x * (0.5 * (1.0 + jnp.tanh(np.sqrt(2.0 / np.pi).astype(np.float32) * (x + 0.044715 * (x * x * x)))))


def _layer_norm(x, g, b):
    xc = x - jnp.mean(x, axis=-1, keepdims=True)
    var = jnp.mean(xc * xc, axis=-1, keepdims=True)
    return xc * lax.rsqrt(var + LN_EPS) * g + b


def _rms_norm(x, g):
    return x * lax.rsqrt(jnp.mean(x * x, axis=-1, keepdims=True) + LN_EPS) * g


def _rope(x, cos, sin_signed, half):
    lane = lax.broadcasted_iota(jnp.int32, (1, LANES), 1)
    first = (lane % (2 * half)) < half
    partner = jnp.where(first, pltpu.roll(x, LANES - half, 1), pltpu.roll(x, half, 1))
    return x * cos + partner * sin_signed


def _softmax_num(scores, extra=None, base2=False):
    ex = jnp.exp2 if base2 else jnp.exp
    s = scores[0] if len(scores) == 1 else jnp.concatenate(scores, axis=1)
    m = jnp.max(s, axis=-1, keepdims=True)
    if extra is None:
        return _bf(ex(s - m)), None
    m = jnp.maximum(m, extra)
    return _bf(ex(s - m)), ex(extra - m)


def _pv_normalised(p, extra_num, values):
    v = values[0] if len(values) == 1 else jnp.concatenate(values, axis=0)
    oe = _dot(p, jnp.concatenate([v, jnp.ones_like(v)], axis=1))
    o, l = oe[:, :LANES], oe[:, LANES:]
    return o / (l if extra_num is None else l + extra_num)


def _softmax_pv(scores, values, extra=None, base2=False):
    return _pv_normalised(*_softmax_num(scores, extra, base2), values)


def _pipelined(n, stages):
    stages = [s if callable(s) else (lambda i, c, fs=s: functools.reduce(lambda acc, f: f(i, acc), fs, c))
              for s in stages]
    depth = len(stages)
    carry = [None] * n
    for tick in range(n + depth - 1):
        for k in range(depth):
            i = tick - k
            if 0 <= i < n:
                carry[i] = stages[k](i, carry[i])


MOD_TN = 1536


def _mod_kernel(c_ref, w_ref, b_ref, o_ref):
    c = c_ref[...]
    s = c * (1.0 / (1.0 + jnp.exp(-c)))
    o_ref[...] = _dot(_bf(s), _bf(w_ref[...])) + b_ref[...]


def _modulation(c16, w_mod, b_mod):
    n = 6 * D_MODEL
    return pl.pallas_call(
        _mod_kernel,
        grid=(DEPTH, n // MOD_TN),
        in_specs=[
            pl.BlockSpec((16, D_MODEL), lambda l, j: (0, 0)),
            pl.BlockSpec((None, D_MODEL, MOD_TN), lambda l, j: (l, 0, j)),
            pl.BlockSpec((None, 1, MOD_TN), lambda l, j: (l, 0, j)),
        ],
        out_specs=pl.BlockSpec((None, 16, MOD_TN), lambda l, j: (l, 0, j)),
        out_shape=jax.ShapeDtypeStruct((DEPTH, 16, n), jnp.float32),
        compiler_params=pltpu.CompilerParams(vmem_limit_bytes=VMEM_LIMIT),
        name="modulation",
    )(c16, w_mod, b_mod.reshape(DEPTH, 1, n))


def _take(w, axis, pieces):
    return jnp.concatenate([lax.slice_in_dim(w, off, off + n, axis=axis) for off, n in pieces], axis=axis)


def _w1_pieces():
    ak, av, bk, bv, cckv, ckr, aq, bq, ccq, du, dv = np.cumsum((0, 128, 128, 256, 256, 128, 32, 256, 256, 256, 256))[:11]
    return ([(aq + 64 * h, 64) for h in (0, 2, 1, 3)]
            + [(bk, 256), (bv, 256), (bq, 256), (ccq, 256), (du, 256), (dv, 256), (ak, 128), (av, 128), (cckv, 128)]
            + [(ckr, 32)] * 4)


_UQ_PIECES = [(96 * h, 64) for h in range(4)] + [(96 * h + 64, 32) for h in range(4)]
_UKV_PIECES = [(128 * h, 64) for h in range(4)] + [(128 * h + 64, 64) for h in range(4)]
_WO_PIECES = [(64 * h, 64) for h in (0, 2, 1, 3)] + [(4 * HEAD_DIM, D_MODEL - 4 * HEAD_DIM)]


def _prep_kernel(wint_ref, wuq_ref, wukv_ref, wout_ref, w1t_ref, uq_ref, ukv_ref, wo_ref):
    w1t_ref[...] = _bf(_take(wint_ref[...], 0, _w1_pieces()))
    wo_ref[...] = _bf(_take(wout_ref[...], 0, _WO_PIECES))
    uq_ref[...] = _bf(_take(wuq_ref[...], 1, _UQ_PIECES))
    ukv_ref[...] = _bf(_take(wukv_ref[...], 1, _UKV_PIECES))


def _prepare_weights(w_in_t, c_w_uq, c_w_ukv, w_out):
    whole = lambda a: pl.BlockSpec((None,) + a.shape[1:], lambda l: (l, 0, 0))
    w1t_shape = jax.ShapeDtypeStruct((DEPTH, W1_W, D_MODEL), jnp.bfloat16)
    outs = [w1t_shape] + [jax.ShapeDtypeStruct(a.shape, jnp.bfloat16) for a in (c_w_uq, c_w_ukv, w_out)]
    return pl.pallas_call(
        _prep_kernel,
        grid=(DEPTH,),
        in_specs=[whole(a) for a in (w_in_t, c_w_uq, c_w_ukv, w_out)],
        out_specs=[whole(a) for a in outs],
        out_shape=outs,
        compiler_params=pltpu.CompilerParams(vmem_limit_bytes=VMEM_LIMIT),
        name="prepare_weights",
    )(w_in_t, c_w_uq, c_w_ukv, w_out)


def _b_step_geometry(step):
    r0 = step * B_QROWS
    w0 = min(max(r0 - B_WIN_H // 2, 0), N_ROWS - B_KROWS)
    return r0, w0


_B_CLASS_STEPS = (0, 1, 2, B_STEPS - 2, B_STEPS - 1)


def _bias_kernel(rpb_ref, o_ref):
    layer = pl.program_id(0)
    n_dr, n_dc = 2 * B_WIN_H - 1, 2 * B_WIN_W - 1
    cq = lax.broadcasted_iota(jnp.int32, (GRID_W, LANES), 0)
    ck = lax.broadcasted_iota(jnp.int32, (GRID_W, LANES), 1) % GRID_W
    c_start = jnp.clip(cq - B_WIN_W // 2, 0, GRID_W - B_WIN_W)
    col_ok = (ck >= c_start) & (ck < c_start + B_WIN_W)
    dc = jnp.clip(ck - cq + (B_WIN_W - 1), 0, n_dc - 1)
    neg = jnp.full((GRID_W, LANES), NEG_INF, jnp.float32)
    lane_lo = lax.broadcasted_iota(jnp.int32, (GRID_W, LANES), 1) < GRID_W
    for h in range(4):
        tz = []
        for dr in range(n_dr):
            t = jnp.zeros((GRID_W, LANES), jnp.float32)
            for d in range(n_dc):
                t = jnp.where(dc == d, rpb_ref[layer, (h * n_dr + dr) * n_dc + d], t)
            tz.append(jnp.where(col_ok, t, neg))
        for ci, step in enumerate(_B_CLASS_STEPS):
            r0, w0 = _b_step_geometry(step)
            for qr in range(B_QROWS):
                r = r0 + qr
                r_start = min(max(r - B_WIN_H // 2, 0), N_ROWS - B_WIN_H)
                for kp in range(B_KROWS // 2):
                    halves = []
                    for kr in (2 * kp, 2 * kp + 1):
                        rk = w0 + kr
                        ok = r_start <= rk < r_start + B_WIN_H
                        halves.append(tz[rk - r + (B_WIN_H - 1)] if ok else neg)
                    o_ref[ci, h, qr * GRID_W:(qr + 1) * GRID_W, kp * LANES:(kp + 1) * LANES] = (
                        jnp.where(lane_lo, halves[0], halves[1]))


def _bias_tables(rpb):
    return pl.pallas_call(
        _bias_kernel,
        grid=(DEPTH,),
        in_specs=[pl.BlockSpec(memory_space=pltpu.SMEM)],
        out_specs=pl.BlockSpec((None, B_CLASSES, 4, B_QBLK, B_KBLK), lambda l: (l, 0, 0, 0, 0)),
        out_shape=jax.ShapeDtypeStruct((DEPTH, B_CLASSES, 4, B_QBLK, B_KBLK), jnp.float32),
        compiler_params=pltpu.CompilerParams(vmem_limit_bytes=VMEM_LIMIT),
        name="b_bias_tables",
    )(rpb.reshape(DEPTH, -1))


def _tile_rows(refs, rows):
    if len(refs) == 1:
        return refs[0][rows, :]
    return jnp.where(pl.program_id(0) >= X_TILES, refs[1][rows, :], refs[0][rows, :])


def _in_kernel(*refs, n_x):
    x_refs, refs = refs[:n_x], refs[n_x:]
    (mod_ref, w1_ref, cosa_ref, sina_ref, cosc_ref, sinc_ref, qn_ref, kvn_ref,
     wuq_ref, wukv_ref, dg_ref, db_ref, ws_ref, bs_ref, wgu_ref, wd_ref, p_ref, wgu_out_ref, wd_out_ref) = refs[:19]
    x_out_refs = refs[19:]
    wgu_out_ref[...] = _bf(wgu_ref[...])
    wd_out_ref[...] = _bf(wd_ref[...])
    shift = mod_ref[:, 0:D_MODEL]
    scale = mod_ref[:, D_MODEL:2 * D_MODEL]
    qscale = HEAD_DIM ** -0.5
    lo = _lane_lt(LANES, HEAD_DIM)

    def projections(c, _):
        rows = slice(c * CHAIN, (c + 1) * CHAIN)
        x_rows = _tile_rows(x_refs, rows)
        for x_out_ref in x_out_refs:
            x_out_ref[rows, :] = x_rows
        h = _bf(x_rows * (1.0 + scale) + shift)
        proj = lambda off: _dot_nt(h, w1_ref[off:off + 256, :])
        cosa, sina = cosa_ref[rows, :], sina_ref[rows, :]
        aq = proj(W1_AQ)
        for t in range(2):
            sl = slice(t * LANES, (t + 1) * LANES)
            p_ref[rows, P_AQ + t * LANES:P_AQ + (t + 1) * LANES] = _bf(_rope(aq[:, sl], cosa, sina, 16) * qscale)
        p_ref[rows, P_BK:P_BK + 256] = _bf(proj(W1_BK))
        p_ref[rows, P_BV:P_BV + 256] = _bf(proj(W1_BV))
        p_ref[rows, P_BQ:P_BQ + 256] = _bf(proj(W1_BQ) * qscale)
        akv = proj(W1_AKV)
        p_ref[rows, P_AK:P_AK + LANES] = _bf(_rope(akv[:, :LANES], cosa, sina, 16))
        p_ref[rows, P_AV:P_AV + LANES] = _bf(akv[:, LANES:])
        return proj(W1_CKV), proj(W1_CCQ), proj(W1_DU), proj(W1_DV)

    def mixer_prep(c, carry):
        rows = slice(c * CHAIN, (c + 1) * CHAIN)
        ckv, ccq, du, dv = carry
        cosc, sinc = cosc_ref[rows, :], sinc_ref[rows, :]
        kr4 = _bf(_rope(ckv[:, LANES:], cosc, sinc, 8))
        kv = _dot(_bf(_rms_norm(ckv[:, :LANES], kvn_ref[...])), wukv_ref[...])
        p_ref[rows, P_CKX0:P_CKX0 + LANES] = _bf(kv[:, 0:LANES])
        p_ref[rows, P_CKX0 + LANES:P_CKX0 + 256] = kr4
        p_ref[rows, P_CKX1:P_CKX1 + LANES] = _bf(kv[:, LANES:2 * LANES])
        p_ref[rows, P_CKX1 + LANES:P_CKX1 + 256] = kr4
        p_ref[rows, P_CV:P_CV + 256] = _bf(kv[:, 256:512])
        qq = _dot(_bf(_rms_norm(ccq, qn_ref[...])), wuq_ref[...])
        p_ref[rows, P_CQN:P_CQN + 256] = _bf(qq[:, 0:256])
        p_ref[rows, P_CQR:P_CQR + LANES] = _bf(_rope(qq[:, 256:384], cosc, sinc, 8))
        u = _gelu(du)
        v = _bf(_layer_norm(_gelu(dv), dg_ref[...], db_ref[...]))
        for ch in range(CHAIN // 128):
            sub = slice(ch * 128, (ch + 1) * 128)
            out_rows = slice(c * CHAIN + ch * 128, c * CHAIN + (ch + 1) * 128)
            for pr in range(2):
                cols = slice(pr * LANES, (pr + 1) * LANES)
                vt = v[sub, cols]
                mixed = jnp.where(lo, _dot(ws_ref[2 * pr], vt), _dot(ws_ref[2 * pr + 1], vt))
                p_ref[out_rows, P_YD + pr * LANES:P_YD + (pr + 1) * LANES] = _bf(
                    u[sub, cols] * (mixed + bs_ref[:, cols]))

    _pipelined(TM // CHAIN, (projections, mixer_prep))


def _lat_tile(i):
    return jnp.minimum(i, X_TILES - 1)


def _ctx_tile(i):
    return jnp.maximum(i - X_TILES, 0)


def _mod_row(i):
    return jnp.where(i < X_TILES, i // (SEQ // TM), BATCH)


def _stream_specs(xs, width):
    if len(xs) == 1:
        return [pl.BlockSpec((TM, width), lambda i: (i, 0))]
    return [pl.BlockSpec((TM, width), lambda i: (_lat_tile(i), 0)),
            pl.BlockSpec((TM, width), lambda i: (_ctx_tile(i), 0))]


def _stage_in(l, xs, tabs, mods, w1, qn, kvn, wuq, wukv, dg, db, ws, bs_exp, ffn_f32):
    full = lambda shape: pl.BlockSpec((None,) + shape, lambda i: (l,) + (0,) * len(shape))
    tab = pl.BlockSpec((TM, LANES), lambda i: (jnp.where(i < X_TILES, i % (SEQ // TM), SEQ // TM), 0))
    in_specs = _stream_specs(xs, D_MODEL) + [
        pl.BlockSpec((None, None, 1, 6 * D_MODEL), lambda i: (l, _mod_row(i), 0, 0)),
        full((W1_W, D_MODEL)),
        tab, tab, tab, tab,
        full((1, C_Q_RANK)), full((1, C_KV_RANK)),
        full((C_Q_RANK, 384)), full((C_KV_RANK, 512)),
        full((1, 256)), full((1, 256)),
        full((4, 128, 128)), full((128, 256)),
    ]
    out_specs = [pl.BlockSpec((TM, P_W), lambda i: (i, 0))]
    out_shape = [jax.ShapeDtypeStruct((NX + NC, P_W), jnp.bfloat16)]
    for w in ffn_f32:
        rows, cols = w.shape[1] // X_TILES, w.shape[2]
        in_specs.append(pl.BlockSpec((None, rows, cols), lambda i: (l, _lat_tile(i), 0)))
        out_specs.append(pl.BlockSpec((rows, cols), lambda i: (_lat_tile(i), 0)))
        out_shape.append(jax.ShapeDtypeStruct(w.shape[1:], jnp.bfloat16))
    if len(xs) == 2:
        out_specs.append(pl.BlockSpec((TM, D_MODEL), lambda i: (i, 0)))
        out_shape.append(jax.ShapeDtypeStruct((NX + NC, D_MODEL), xs[0].dtype))
    return pl.pallas_call(
        functools.partial(_in_kernel, n_x=len(xs)),
        grid=(ALL_TILES,),
        in_specs=in_specs,
        out_specs=out_specs,
        out_shape=out_shape,
        compiler_params=pltpu.CompilerParams(vmem_limit_bytes=VMEM_LIMIT),
        name="stage_in",
    )(*xs, mods, w1, *tabs, qn, kvn, wuq, wukv, dg, db, ws, bs_exp, *ffn_f32)


def _mask_half(q_tile, c):
    return jnp.where(_half_mask(c), q_tile, jnp.zeros_like(q_tile))


def _stack_pair(q_tile):
    return jnp.concatenate([_mask_half(q_tile, 0), _mask_half(q_tile, 1)], axis=0)


def _unstack_pair(o):
    n = o.shape[0] // 2
    return jnp.where(_lane_lt(LANES, HEAD_DIM), o[:n], o[n:])


def _sink_column(sink_ref, heads, n):
    return jnp.concatenate([jnp.full((n, 1), sink_ref[h], jnp.float32) for h in heads], axis=0)


def _a_kernel(sink_ref, q_ref, k_ref, v_ref, kc_ref, vc_ref, o_ref):
    g = pl.program_id(1)
    kc, vc = kc_ref[...], vc_ref[...]
    sink = _sink_column(sink_ref, (0, 2, 1, 3), A_BLK)

    def band_start(j):
        return pl.multiple_of(jnp.clip((g * A_SUB + j - 1) * A_BLK, 0, SEQ - A_BAND), A_BLK)

    def scores(j, _):
        rows = slice(j * A_BLK, (j + 1) * A_BLK)
        start = band_start(j)
        qpos = (g * A_SUB + j) * A_BLK + lax.broadcasted_iota(jnp.int32, (A_BLK, A_BAND), 0)
        kpos = start + lax.broadcasted_iota(jnp.int32, (A_BLK, A_BAND), 1)
        valid = jnp.abs(qpos - kpos) <= A_WINDOW
        qs = jnp.concatenate([_stack_pair(q_ref[rows, 0:LANES]), _stack_pair(q_ref[rows, LANES:256])], axis=0)
        s = _dot_nt(qs, k_ref[pl.ds(start, A_BAND), :]).reshape(4, A_BLK, A_BAND)
        s = jnp.where(valid[None], s, NEG_INF).reshape(4 * A_BLK, A_BAND)
        return s, _dot_nt(qs, kc)

    def softmax(j, s):
        return _softmax_num(s, extra=sink)

    def finish(j, num):
        rows = slice(j * A_BLK, (j + 1) * A_BLK)
        o = _pv_normalised(*num, (v_ref[pl.ds(band_start(j), A_BAND), :], vc))
        o_ref[rows, 0:LANES] = _bf(_unstack_pair(o[:2 * A_BLK]))
        o_ref[rows, LANES:256] = _bf(_unstack_pair(o[2 * A_BLK:]))

    _pipelined(A_SUB, (scores, (softmax, finish)))


CTX0 = NX // CTX_LEN


def _mixer_a(p, sink_l):
    ng = SEQ // (A_BLK * A_SUB)
    return pl.pallas_call(
        _a_kernel,
        grid=(BATCH, ng),
        in_specs=[
            pl.BlockSpec(memory_space=pltpu.SMEM),
            pl.BlockSpec((A_BLK * A_SUB, 256), lambda b, n: (b * ng + n, P_AQ // 256)),
            pl.BlockSpec((SEQ, LANES), lambda b, n: (b, P_AK // LANES)),
            pl.BlockSpec((SEQ, LANES), lambda b, n: (b, P_AV // LANES)),
            pl.BlockSpec((CTX_LEN, LANES), lambda b, n: (CTX0 + b, P_AK // LANES)),
            pl.BlockSpec((CTX_LEN, LANES), lambda b, n: (CTX0 + b, P_AV // LANES)),
        ],
        out_specs=pl.BlockSpec((A_BLK * A_SUB, 256), lambda b, n: (b * ng + n, 0)),
        out_shape=jax.ShapeDtypeStruct((NX, 256), jnp.bfloat16),
        compiler_params=pltpu.CompilerParams(vmem_limit_bytes=VMEM_LIMIT),
        name="mixer_a",
    )(sink_l, p, p, p, p, p)


def _b_class(step):
    return jnp.where(step < 2, step, jnp.where(step >= B_STEPS - 2, step - (B_STEPS - B_CLASSES), 2))


def _b_kernel(q_ref, k_ref, v_ref, kc_ref, vc_ref, bias_ref, o_ref):
    g = pl.program_id(1)

    def where(i):
        j, t = i // 2, i % 2
        step = g * B_SUB + j
        w0 = jnp.clip(step * B_QROWS - B_WIN_H // 2, 0, N_ROWS - B_KROWS)
        return (step, slice(j * B_QBLK, (j + 1) * B_QBLK), slice(t * LANES, (t + 1) * LANES), t,
                pl.multiple_of(w0 * GRID_W, GRID_W))

    def scores(i, _):
        step, rows, cols, t, start = where(i)
        qs = _stack_pair(q_ref[rows, cols])
        bias = bias_ref[_b_class(step), 2 * t:2 * t + 2].reshape(2 * B_QBLK, B_KBLK)
        return _dot_nt(qs, k_ref[pl.ds(start, B_KBLK), cols]) + bias, _dot_nt(qs, kc_ref[:, cols])

    def softmax(i, s):
        return _softmax_num(s)

    def finish(i, num):
        _, rows, cols, _, start = where(i)
        o = _pv_normalised(*num, (v_ref[pl.ds(start, B_KBLK), cols], vc_ref[:, cols]))
        o_ref[rows, cols] = _bf(_unstack_pair(o))

    _pipelined(2 * B_SUB, (scores, (softmax, finish)))


def _mixer_b(l, p, bias_tabs):
    ng = B_STEPS // B_SUB
    return pl.pallas_call(
        _b_kernel,
        grid=(BATCH, ng),
        in_specs=[
            pl.BlockSpec((B_QBLK * B_SUB, 256), lambda b, s: (b * ng + s, P_BQ // 256)),
            pl.BlockSpec((SEQ, 256), lambda b, s: (b, P_BK // 256)),
            pl.BlockSpec((SEQ, 256), lambda b, s: (b, P_BV // 256)),
            pl.BlockSpec((CTX_LEN, 256), lambda b, s: (CTX0 + b, P_BK // 256)),
            pl.BlockSpec((CTX_LEN, 256), lambda b, s: (CTX0 + b, P_BV // 256)),
            pl.BlockSpec((None, B_CLASSES, 4, B_QBLK, B_KBLK), lambda b, s: (l, 0, 0, 0, 0)),
        ],
        out_specs=pl.BlockSpec((B_QBLK * B_SUB, 256), lambda b, s: (b * ng + s, 0)),
        out_shape=jax.ShapeDtypeStruct((NX, 256), jnp.bfloat16),
        compiler_params=pltpu.CompilerParams(vmem_limit_bytes=VMEM_LIMIT),
        name="mixer_b",
    )(p, p, p, p, p, bias_tabs)


C_SCALE_LOG2E = float((C_NOPE + C_ROPE) ** -0.5 * np.log2(np.e))


def _c_query(qn_tile, qr_tile, h):
    lane = lax.broadcasted_iota(jnp.int32, (1, LANES), 1)
    quarter = (lane >= h * C_ROPE) & (lane < (h + 1) * C_ROPE)
    return jnp.concatenate([_mask_half(qn_tile, h % 2),
                            jnp.where(quarter, qr_tile, jnp.zeros_like(qr_tile))], axis=1)


def _c_pair_queries(qn_tile, qr_tile, t):
    return jnp.concatenate([_c_query(qn_tile, qr_tile, 2 * t), _c_query(qn_tile, qr_tile, 2 * t + 1)], axis=0)


def _c_kernel(qn_ref, qr_ref, kx0_ref, kx1_ref, v_ref, kx0c_ref, kx1c_ref, vc_ref, o_ref):
    kx_refs = ((kx0_ref, kx0c_ref), (kx1_ref, kx1c_ref))

    def where(i):
        j, t = i // 2, i % 2
        return slice(j * C_SUBBLK, (j + 1) * C_SUBBLK), slice(t * LANES, (t + 1) * LANES), t

    def scores(i, _):
        rows, cols, t = where(i)
        qs = _c_pair_queries(qn_ref[rows, cols], qr_ref[rows, :], t)
        return tuple(_dot_nt(qs, r[...]) * C_SCALE_LOG2E for r in kx_refs[t])

    def softmax(i, s):
        return _softmax_num(s, base2=True)

    def finish(i, num):
        rows, cols, _ = where(i)
        o_ref[rows, cols] = _bf(_unstack_pair(_pv_normalised(*num, (v_ref[:, cols], vc_ref[:, cols]))))

    _pipelined(2 * (C_QBLK // C_SUBBLK), (scores, (softmax, finish)))


def _mixer_c(p):
    nq = SEQ // C_QBLK
    return pl.pallas_call(
        _c_kernel,
        grid=(BATCH, nq),
        in_specs=[
            pl.BlockSpec((C_QBLK, 256), lambda b, n: (b * nq + n, P_CQN // 256)),
            pl.BlockSpec((C_QBLK, LANES), lambda b, n: (b * nq + n, P_CQR // LANES)),
            pl.BlockSpec((SEQ, 256), lambda b, n: (b, P_CKX0 // 256)),
            pl.BlockSpec((SEQ, 256), lambda b, n: (b, P_CKX1 // 256)),
            pl.BlockSpec((SEQ, 256), lambda b, n: (b, P_CV // 256)),
            pl.BlockSpec((CTX_LEN, 256), lambda b, n: (CTX0 + b, P_CKX0 // 256)),
            pl.BlockSpec((CTX_LEN, 256), lambda b, n: (CTX0 + b, P_CKX1 // 256)),
            pl.BlockSpec((CTX_LEN, 256), lambda b, n: (CTX0 + b, P_CV // 256)),
        ],
        out_specs=pl.BlockSpec((C_QBLK, 256), lambda b, n: (b * nq + n, 0)),
        out_shape=jax.ShapeDtypeStruct((NX, 256), jnp.bfloat16),
        compiler_params=pltpu.CompilerParams(vmem_limit_bytes=VMEM_LIMIT),
        name="mixer_c",
    )(p, p, p, p, p, p, p, p)


def _ctx_kernel(sink_ref, aq_ref, ak_ref, av_ref, bq_ref, bk_ref, bv_ref, cqn_ref, cqr_ref,
                kx0_ref, kx1_ref, cv_ref, ya_ref, yb_ref, yc_ref):
    for e in range(CTX_PER_STEP):
        rows = slice(e * CTX_LEN, (e + 1) * CTX_LEN)
        ak, av = ak_ref[rows, :], av_ref[rows, :]
        cqr = cqr_ref[rows, :]
        for t in range(2):
            cols = slice(t * LANES, (t + 1) * LANES)
            qs = _stack_pair(aq_ref[rows, cols])
            o = _softmax_pv((_dot_nt(qs, ak),), (av,), extra=_sink_column(sink_ref, (t, t + 2), CTX_LEN))
            ya_ref[rows, cols] = _bf(_unstack_pair(o))
            qs = _stack_pair(bq_ref[rows, cols])
            yb_ref[rows, cols] = _bf(_unstack_pair(
                _softmax_pv((_dot_nt(qs, bk_ref[rows, cols]),), (bv_ref[rows, cols],))))
            qs = _c_pair_queries(cqn_ref[rows, cols], cqr, t)
            s = _dot_nt(qs, (kx0_ref, kx1_ref)[t][rows, :]) * C_SCALE_LOG2E
            yc_ref[rows, cols] = _bf(_unstack_pair(_softmax_pv((s,), (cv_ref[rows, cols],), base2=True)))


CTX_PER_STEP = 2


def _mixers_ctx(p, sink_l):
    n = CTX_PER_STEP * CTX_LEN
    blk = lambda w, off: pl.BlockSpec((n, w), lambda g: (NX // n + g, off // w))
    yspec = pl.BlockSpec((n, 256), lambda g: (g, 0))
    yshape = jax.ShapeDtypeStruct((NC, 256), jnp.bfloat16)
    return pl.pallas_call(
        _ctx_kernel,
        grid=(BATCH // CTX_PER_STEP,),
        in_specs=[
            pl.BlockSpec(memory_space=pltpu.SMEM),
            blk(256, P_AQ), blk(LANES, P_AK), blk(LANES, P_AV),
            blk(256, P_BQ), blk(256, P_BK), blk(256, P_BV),
            blk(256, P_CQN), blk(LANES, P_CQR), blk(256, P_CKX0), blk(256, P_CKX1), blk(256, P_CV),
        ],
        out_specs=[yspec, yspec, yspec],
        out_shape=[yshape, yshape, yshape],
        compiler_params=pltpu.CompilerParams(vmem_limit_bytes=VMEM_LIMIT),
        name="mixers_ctx",
    )(sink_l, *([p] * 11))


def _post_kernel(*refs, n_x, n_y):
    x_refs, refs = refs[:n_x], refs[n_x:]
    mod_ref, refs = refs[0], refs[1:]
    y_refs, refs = [refs[k * n_y:(k + 1) * n_y] for k in range(3)], refs[3 * n_y:]
    yd_ref, wo_ref, g1_ref, b1_ref, wgu_ref, wd_ref, g2_ref, b2_ref, o_ref = refs
    m = lambda j: mod_ref[:, j * D_MODEL:(j + 1) * D_MODEL]

    def mixer_out(c, _):
        rows = slice(c * POST_CHAIN, (c + 1) * POST_CHAIN)
        ycat = jnp.concatenate([_tile_rows(y, rows) for y in y_refs] + [yd_ref[rows, :]], axis=1)
        y = _dot(ycat, wo_ref[...])
        x1 = _layer_norm(DN_ALPHA * _tile_rows(x_refs, rows) + m(2) * y, g1_ref[...], b1_ref[...])
        return x1, _bf(x1 * (1.0 + m(4)) + m(3))

    def ffn(c, carry):
        x1, h = carry
        acc = jnp.zeros((POST_CHAIN, D_MODEL), jnp.float32)
        for j in range(FF_HIDDEN // FF_CHUNK):
            g = _dot(h, wgu_ref[:, j * FF_CHUNK:(j + 1) * FF_CHUNK])
            u = _dot(h, wgu_ref[:, FF_HIDDEN + j * FF_CHUNK:FF_HIDDEN + (j + 1) * FF_CHUNK])
            a = _bf(g * (1.0 / (1.0 + jnp.exp(-g))) * u)
            acc = acc + _dot(a, wd_ref[j * FF_CHUNK:(j + 1) * FF_CHUNK, :])
        return x1, acc

    def norm_out(c, carry):
        x1, acc = carry
        o_ref[c * POST_CHAIN:(c + 1) * POST_CHAIN, :] = _layer_norm(
            DN_ALPHA * x1 + m(5) * acc, g2_ref[...], b2_ref[...])

    _pipelined(TM // POST_CHAIN, (mixer_out, ffn, norm_out))


def _stage_post(l, n_tiles, xs, ys, mods, p, wo, g1, b1, wgu, wd, g2, b2):
    full = lambda shape: pl.BlockSpec((None,) + shape, lambda i: (l,) + (0,) * len(shape))
    y_specs = [spec for y in ys for spec in _stream_specs(y, 256)]
    return pl.pallas_call(
        functools.partial(_post_kernel, n_x=len(xs), n_y=len(ys[0])),
        grid=(n_tiles,),
        in_specs=_stream_specs(xs, D_MODEL) + [
            pl.BlockSpec((None, None, 1, 6 * D_MODEL), lambda i: (l, _mod_row(i), 0, 0)),
            *y_specs,
            pl.BlockSpec((TM, 256), lambda i: (i, P_YD // 256)),
            full((D_MODEL, D_MODEL)), full((1, D_MODEL)), full((1, D_MODEL)),
            pl.BlockSpec((D_MODEL, 2 * FF_HIDDEN), lambda i: (0, 0)), pl.BlockSpec((FF_HIDDEN, D_MODEL), lambda i: (0, 0)),
            full((1, D_MODEL)), full((1, D_MODEL)),
        ],
        out_specs=pl.BlockSpec((TM, D_MODEL), lambda i: (i, 0)),
        out_shape=jax.ShapeDtypeStruct((n_tiles * TM, D_MODEL), jnp.float32),
        compiler_params=pltpu.CompilerParams(vmem_limit_bytes=VMEM_LIMIT),
        name="stage_post",
    )(*xs, mods, *[a for y in ys for a in y], p, wo, g1, b1, wgu, wd, g2, b2)


def _rope_tables():
    t = lax.broadcasted_iota(jnp.int32, (SEQ, LANES), 0)
    grid_row, grid_col = (t // GRID_W).astype(jnp.float32), (t % GRID_W).astype(jnp.float32)

    def table(width):
        half, quarter = width // 2, width // 4
        lane = np.arange(LANES) % width
        k = (lane % half) % quarter
        inv = (np.float32(ROPE_BASE) ** (-(2 * k).astype(np.float32) / np.float32(half)))[None, :]
        on_cols = (lane // half == 1)[None, :]
        first = ((lane % half) < quarter)[None, :]
        ang = jnp.where(on_cols, grid_col, grid_row) * inv
        sin = jnp.sin(ang)
        return jnp.cos(ang), jnp.where(first, -sin, sin)

    return (*table(HEAD_DIM), *table(C_ROPE))


def kernel(x, c, ctx, c_ctx, w_mod, b_mod, w_in, a_sink, b_rpb, c_q_norm, c_kv_norm, c_w_uq, c_w_ukv,
           d_ln_g, d_ln_b, d_ws, d_bs, w_out, ln1_g, ln1_b, w_gu, w_down, ln2_g, ln2_b):
    f32 = jnp.float32
    c16 = jnp.concatenate([c, c_ctx[None, :], jnp.zeros((16 - BATCH - 1, D_MODEL), f32)], axis=0)
    mods = _modulation(c16, w_mod, b_mod).reshape(DEPTH, 16, 1, 6 * D_MODEL)

    w1, wuq, wukv, wo = _prepare_weights(jnp.swapaxes(w_in, 1, 2), c_w_uq, c_w_ukv, w_out)
    ws = _bf(d_ws)
    bs_exp = jnp.repeat(jnp.swapaxes(d_bs, 1, 2), HEAD_DIM, axis=2)
    row = lambda a: a[:, None, :]
    identity = (jnp.ones((TM, LANES), f32), jnp.zeros((TM, LANES), f32)) * 2
    tabs = [jnp.concatenate([t, e], axis=0) for t, e in zip(_rope_tables(), identity)]

    bias_tabs = _bias_tables(b_rpb)

    xs = (x.reshape(NX, D_MODEL), ctx.reshape(NC, D_MODEL))
    for l in range(DEPTH):
        last = l == DEPTH - 1
        p, wgu, wd, *joined = _stage_in(l, xs, tabs, mods, w1, row(c_q_norm), row(c_kv_norm), wuq, wukv,
                                        row(d_ln_g), row(d_ln_b), ws, bs_exp, (w_gu, w_down))
        xs = tuple(joined) or xs
        ys = (_mixer_a(p, a_sink[l]), _mixer_b(l, p, bias_tabs), _mixer_c(p))
        if last:
            ys = tuple((y,) for y in ys)
        else:
            ys = tuple(zip(ys, _mixers_ctx(p, a_sink[l])))
        xs = (_stage_post(l, X_TILES if last else ALL_TILES, xs, ys, mods, p, wo, row(ln1_g), row(ln1_b),
                          wgu, wd, row(ln2_g), row(ln2_b)),)
    return xs[0].reshape(BATCH, SEQ, D_MODEL)
```
